```python
import jax, jax.numpy as jnp
from jax import lax
import numpy as np


D_MODEL = 2048
BATCH = 8
SEQ = 4096
DEPTH = 2
DEC_BATCH = 1
DEC_SEQ = 16384
PAST_LEN = 128

N_GROUPS = 4
GROUP_DIM = 128
MIX_W = N_GROUPS * GROUP_DIM
POOL_WINDOWS = (2, 4, 8, 16)
GDN_HEADS = 4
GDN_DK = 128
GDN_DV = 128
GDN_CONV = 4
GDN_CHUNK = 64
CONF_WIDTH = 31
SGU_CHUNK = 128
D_FF = 4 * D_MODEL
N_BRANCH = 4
EPS = 1e-6

D_POOL_IN = MIX_W
D_GDN_QKV = 3 * MIX_W
D_GDN_Z = MIX_W
D_GDN_AB = 4 * GDN_HEADS
D_CONF_IN = 2 * MIX_W
D_SGU_IN = 2 * MIX_W
D_GATE = N_BRANCH * D_MODEL
D_IN = D_POOL_IN + D_GDN_QKV + D_GDN_Z + D_GDN_AB + D_CONF_IN + D_SGU_IN + D_GATE
SPLITS = (D_POOL_IN,
          D_POOL_IN + D_GDN_QKV,
          D_POOL_IN + D_GDN_QKV + D_GDN_Z,
          D_POOL_IN + D_GDN_QKV + D_GDN_Z + D_GDN_AB,
          D_POOL_IN + D_GDN_QKV + D_GDN_Z + D_GDN_AB + D_CONF_IN,
          D_POOL_IN + D_GDN_QKV + D_GDN_Z + D_GDN_AB + D_CONF_IN + D_SGU_IN)

kernel_name = 'hybrid_bidir_encoder'


def rmsnorm(x, w):
    x32 = x.astype(jnp.float32)
    y = x32 * lax.rsqrt(jnp.mean(x32 * x32, axis=-1, keepdims=True) + EPS)
    return (y * w.astype(jnp.float32)).astype(x.dtype)


def layernorm(x, w, b):
    x32 = x.astype(jnp.float32)
    mu = jnp.mean(x32, axis=-1, keepdims=True)
    var = jnp.mean(jnp.square(x32 - mu), axis=-1, keepdims=True)
    y = (x32 - mu) * lax.rsqrt(var + EPS)
    return (y * w.astype(jnp.float32) + b.astype(jnp.float32)).astype(x.dtype)


def l2norm(x):
    return x * lax.rsqrt(jnp.sum(x * x, axis=-1, keepdims=True) + EPS)


def depthwise_conv(x, w):
    k, c = w.shape
    pad_l = (k - 1) // 2
    return lax.conv_general_dilated(x, w[:, None, :].astype(x.dtype), (1,), [(pad_l, k - 1 - pad_l)],
                                    dimension_numbers=('NWC', 'WIO', 'NWC'), feature_group_count=c)


def pool_mixer(xa, pool_w, pool_scale):
    b, s, _ = xa.shape
    xg = xa.astype(jnp.float32).reshape(b, s, N_GROUPS, GROUP_DIM)
    cs = jnp.pad(jnp.cumsum(xg, axis=1), ((0, 0), (1, 0), (0, 0), (0, 0)))
    win = jnp.array(POOL_WINDOWS, jnp.int32)[None, :]
    t = jnp.arange(s, dtype=jnp.int32)[:, None]
    lo = jnp.clip(t - win // 2, 0, s)
    hi = jnp.clip(t - win // 2 + win, 0, s)
    grp = jnp.arange(N_GROUPS, dtype=jnp.int32)[None, :]
    cnt = (hi - lo).astype(jnp.float32)[None, :, :, None]
    mean = (cs[:, hi, grp] - cs[:, lo, grp]) / cnt
    y = jnp.einsum('bsgc,gcd->bsgd', mean - xg, pool_w.astype(jnp.float32))
    return (y.reshape(b, s, MIX_W) * pool_scale.astype(jnp.float32)).astype(xa.dtype)


def chunk_gated_delta(q, k, v, g, beta):
    b, s, h, dk = q.shape
    dv = v.shape[-1]
    n = s // GDN_CHUNK
    q = l2norm(q) * (dk ** -0.5)
    k = l2norm(k)
    q, k, v = [t.transpose(0, 2, 1, 3).reshape(b, h, n, GDN_CHUNK, -1) for t in (q, k, v)]
    g = g.transpose(0, 2, 1).reshape(b, h, n, GDN_CHUNK)
    beta = beta.transpose(0, 2, 1).reshape(b, h, n, GDN_CHUNK)
    gc = jnp.cumsum(g, axis=-1)
    tri = jnp.tril(jnp.ones((GDN_CHUNK, GDN_CHUNK), bool))
    strict = jnp.tril(jnp.ones((GDN_CHUNK, GDN_CHUNK), bool), -1)
    decay = jnp.exp(jnp.where(tri, gc[..., :, None] - gc[..., None, :], -jnp.inf))
    kb = k * beta[..., None]
    vb = v * beta[..., None]
    lmat = jnp.where(strict, jnp.einsum('bhnid,bhnjd->bhnij', kb, k) * decay, 0.0)
    u = lax.linalg.triangular_solve(lmat, vb, left_side=True, lower=True, unit_diagonal=True)
    w = lax.linalg.triangular_solve(lmat, kb * jnp.exp(gc)[..., None], left_side=True, lower=True,
                                    unit_diagonal=True)
    qk = jnp.einsum('bhnid,bhnjd->bhnij', q, k) * decay

    def step(state, inp):
        qn, kn, un, wn, gn, qkn = inp
        v_new = un - jnp.einsum('bhcd,bhde->bhce', wn, state)
        o = (jnp.einsum('bhcd,bhde->bhce', qn * jnp.exp(gn)[..., None], state)
             + jnp.einsum('bhij,bhje->bhie', qkn, v_new))
        g_last = gn[..., -1]
        state = (state * jnp.exp(g_last)[..., None, None]
                 + jnp.einsum('bhcd,bhce->bhde', kn * jnp.exp(g_last[..., None] - gn)[..., None], v_new))
        return state, o

    xs = tuple(jnp.moveaxis(t, 2, 0) for t in (q, k, u, w, gc, qk))
    state0 = jnp.zeros((b, h, dk, dv), jnp.float32)
    _, o = lax.scan(step, state0, xs)
    return o.transpose(1, 0, 3, 2, 4).reshape(b, s, h, dv)


def gdn_mixer(xqkv, xz, xab, conv_w, a_log, dt_bias, norm_w):
    b, s, _ = xqkv.shape
    qkv = jax.nn.silu(depthwise_conv(xqkv, conv_w)).astype(jnp.float32)
    q, k, v = jnp.split(qkv, 3, axis=-1)
    q = q.reshape(b, s, GDN_HEADS, GDN_DK)
    k = k.reshape(b, s, GDN_HEADS, GDN_DK)
    v = v.reshape(b, s, GDN_HEADS, GDN_DV)
    ab = xab.astype(jnp.float32).reshape(b, s, 2, 2, GDN_HEADS)
    beta = jax.nn.sigmoid(ab[:, :, 0])
    g = -jnp.exp(a_log.astype(jnp.float32)) * jax.nn.softplus(ab[:, :, 1] + dt_bias.astype(jnp.float32))
    o_f = chunk_gated_delta(q, k, v, g[:, :, 0], beta[:, :, 0])
    o_b = chunk_gated_delta(q[:, ::-1], k[:, ::-1], v[:, ::-1], g[:, ::-1, 1], beta[:, ::-1, 1])[:, ::-1]
    o = o_f + o_b
    o = o * lax.rsqrt(jnp.mean(o * o, axis=-1, keepdims=True) + EPS) * norm_w.astype(jnp.float32)
    o = o * jax.nn.silu(xz.astype(jnp.float32).reshape(b, s, GDN_HEADS, GDN_DV))
    return o.reshape(b, s, MIX_W).astype(xqkv.dtype)


def conformer_conv(xc, dw_w, dw_b, ln_w, ln_b):
    a, gt = jnp.split(xc, 2, axis=-1)
    h = a * jax.nn.sigmoid(gt)
    h = depthwise_conv(h, dw_w) + dw_b
    return jax.nn.silu(layernorm(h, ln_w, ln_b))


def spatial_gating(xd, ln_w, ln_b, w_s, b_s):
    b, s, _ = xd.shape
    u, v = jnp.split(jax.nn.gelu(xd), 2, axis=-1)
    v = layernorm(v, ln_w, ln_b).reshape(b, s // SGU_CHUNK, SGU_CHUNK, N_GROUPS, GROUP_DIM)
    sv = jnp.einsum('gij,bnjgc->bnigc', w_s, v) + b_s.T[:, :, None]
    return u * sv.reshape(b, s, MIX_W)


def encoder(x, norm_mix_w, w_in, pool_w, pool_scale, pool_proj, gdn_conv_w, gdn_a_log, gdn_dt_bias,
            gdn_norm_w, gdn_proj, conf_dw_w, conf_dw_b, conf_ln_w, conf_ln_b, conf_proj, sgu_ln_w, sgu_ln_b,
            sgu_w, sgu_b, sgu_proj, w_out, norm_mlp_w, mlp_w1, mlp_w2, norm_final_w):
    b, s, _ = x.shape
    for l in range(DEPTH):
        h = rmsnorm(x, norm_mix_w[l])
        xa, xqkv, xz, xab, xc, xd, xg = jnp.split(h @ w_in[l], SPLITS, axis=-1)
        ya = pool_mixer(xa, pool_w[l], pool_scale[l]) @ pool_proj[l]
        yb = gdn_mixer(xqkv, xz, xab, gdn_conv_w[l], gdn_a_log[l], gdn_dt_bias[l], gdn_norm_w[l]) @ gdn_proj[l]
        yc = conformer_conv(xc, conf_dw_w[l], conf_dw_b[l], conf_ln_w[l], conf_ln_b[l]) @ conf_proj[l]
        yd = spatial_gating(xd, sgu_ln_w[l], sgu_ln_b[l], sgu_w[l], sgu_b[l]) @ sgu_proj[l]
        gate = jax.nn.sigmoid(xg.astype(jnp.float32)).astype(x.dtype).reshape(b, s, N_BRANCH, D_MODEL)
        merged = gate[:, :, 0] * ya + gate[:, :, 1] * yb + gate[:, :, 2] * yc + gate[:, :, 3] * yd
        x = x + merged @ w_out[l]
        h = rmsnorm(x, norm_mlp_w[l])
        x = x + jnp.square(jax.nn.relu(h @ mlp_w1[l])) @ mlp_w2[l]
    return rmsnorm(x, norm_final_w)


def setup_inputs(seed: int = 0) -> dict:
    key = jax.random.key(seed)
    ks = jax.random.split(key, 32)

    def nrm(k, shape, scale):
        return jax.random.normal(k, shape, jnp.float32) * scale

    def gain(k, shape):
        return 1.0 + 0.1 * jax.random.normal(k, shape, jnp.float32)

    dt = jnp.exp(jax.random.uniform(ks[9], (DEPTH, 2, GDN_HEADS), jnp.float32,
                                    minval=float(np.log(1e-3)), maxval=float(np.log(1e-1))))
    return {
        'x_prompt': nrm(ks[0], (BATCH, SEQ, D_MODEL), 1.0),
        'x_sample': nrm(ks[1], (DEC_BATCH, DEC_SEQ, D_MODEL), 1.0),
        'norm_mix_w': gain(ks[2], (DEPTH, D_MODEL)),
        'w_in': nrm(ks[3], (DEPTH, D_MODEL, D_IN), D_MODEL ** -0.5),
        'pool_w': nrm(ks[4], (DEPTH, N_GROUPS, GROUP_DIM, GROUP_DIM), GROUP_DIM ** -0.5),
        'pool_scale': gain(ks[5], (DEPTH, MIX_W)),
        'pool_proj': nrm(ks[6], (DEPTH, MIX_W, D_MODEL), MIX_W ** -0.5),
        'gdn_conv_w': nrm(ks[7], (DEPTH, GDN_CONV, D_GDN_QKV), GDN_CONV ** -0.5),
        'gdn_a_log': jnp.log(jax.random.uniform(ks[8], (DEPTH, 2, GDN_HEADS), jnp.float32, minval=1.0, maxval=16.0)),
        'gdn_dt_bias': jnp.log(jnp.expm1(dt)),
        'gdn_norm_w': gain(ks[10], (DEPTH, GDN_DV)),
        'gdn_proj': nrm(ks[11], (DEPTH, MIX_W, D_MODEL), MIX_W ** -0.5),
        'conf_dw_w': nrm(ks[12], (DEPTH, CONF_WIDTH, MIX_W), CONF_WIDTH ** -0.5),
        'conf_dw_b': nrm(ks[13], (DEPTH, MIX_W), 0.02),
        'conf_ln_w': gain(ks[14], (DEPTH, MIX_W)),
        'conf_ln_b': nrm(ks[15], (DEPTH, MIX_W), 0.02),
        'conf_proj': nrm(ks[16], (DEPTH, MIX_W, D_MODEL), MIX_W ** -0.5),
        'sgu_ln_w': gain(ks[17], (DEPTH, MIX_W)),
        'sgu_ln_b': nrm(ks[18], (DEPTH, MIX_W), 0.02),
        'sgu_w': nrm(ks[19], (DEPTH, N_GROUPS, SGU_CHUNK, SGU_CHUNK), SGU_CHUNK ** -0.5),
        'sgu_b': gain(ks[20], (DEPTH, N_GROUPS, SGU_CHUNK)),
        'sgu_proj': nrm(ks[21], (DEPTH, MIX_W, D_MODEL), MIX_W ** -0.5),
        'w_out': nrm(ks[22], (DEPTH, D_MODEL, D_MODEL), D_MODEL ** -0.5),
        'norm_mlp_w': gain(ks[23], (DEPTH, D_MODEL)),
        'mlp_w1': nrm(ks[24], (DEPTH, D_MODEL, D_FF), D_MODEL ** -0.5),
        'mlp_w2': nrm(ks[25], (DEPTH, D_FF, D_MODEL), D_FF ** -0.5),
        'norm_final_w': gain(ks[26], (D_MODEL,)),
    }


def reference(x_prompt, x_sample, norm_mix_w, w_in, pool_w, pool_scale, pool_proj, gdn_conv_w, gdn_a_log,
              gdn_dt_bias, gdn_norm_w, gdn_proj, conf_dw_w, conf_dw_b, conf_ln_w, conf_ln_b, conf_proj,
              sgu_ln_w, sgu_ln_b, sgu_w, sgu_b, sgu_proj, w_out, norm_mlp_w, mlp_w1, mlp_w2, norm_final_w):
    params = (norm_mix_w, w_in, pool_w, pool_scale, pool_proj, gdn_conv_w, gdn_a_log, gdn_dt_bias, gdn_norm_w,
              gdn_proj, conf_dw_w, conf_dw_b, conf_ln_w, conf_ln_b, conf_proj, sgu_ln_w, sgu_ln_b, sgu_w, sgu_b,
              sgu_proj, w_out, norm_mlp_w, mlp_w1, mlp_w2, norm_final_w)
    y_prompt = encoder(x_prompt, *params)
    y_sample = encoder(x_sample, *params)
    return (y_prompt, y_sample)
```

```python
import functools

import jax
import jax.numpy as jnp
from jax import lax
from jax.experimental import pallas as pl
from jax.experimental.pallas import tpu as pltpu

F32 = jnp.float32
BF16 = jnp.bfloat16

D_MODEL = 2048
N_GROUPS = 4
GROUP_DIM = 128
MIX_W = N_GROUPS * GROUP_DIM
POOL_WINDOWS = (2, 4, 8, 16)
GDN_HEADS = 4
GDN_DK = 128
GDN_CONV = 4
GDN_CHUNK = 64
CONF_WIDTH = 31
SGU_CHUNK = 128
D_FF = 4 * D_MODEL
N_BRANCH = 4
EPS = 1e-6

LANE = 128
HALO = 16
GDN_BLOCK = 2 * GDN_CHUNK

OFF_QKV = 0
OFF_XA = 3 * MIX_W
OFF_XC = 4 * MIX_W
OFF_XD = 6 * MIX_W
OFF_XZ = 8 * MIX_W
OFF_AB = 9 * MIX_W
D_MIXP = 10 * MIX_W

VMEM_LIMIT = 60 * 1024 * 1024


def _cparams(sem):
    return pltpu.CompilerParams(dimension_semantics=sem, vmem_limit_bytes=VMEM_LIMIT)


def _sigmoid(x):
    return 1.0 / (1.0 + jnp.exp(-x))


def _bdot(a, b):
    return jnp.dot(a.astype(BF16), b.astype(BF16), preferred_element_type=F32)


def _inproj_kernel(x_ref, nw_ref, w_ref, p_ref, h_ref, hs_ref):
    @pl.when(pl.program_id(1) == 0)
    def _():
        x = x_ref[...]
        ms = jnp.mean(x * x, axis=-1, keepdims=True)
        h = ((x * lax.rsqrt(ms + EPS)) * nw_ref[...]).astype(BF16)
        hs_ref[...] = h
        h_ref[...] = h

    p_ref[...] = jnp.dot(hs_ref[...], w_ref[...], preferred_element_type=F32)


def _in_proj(x2d, norm_w, w_mix, tm=512, tn=1024):
    t = x2d.shape[0]
    return pl.pallas_call(
        _inproj_kernel,
        grid=(t // tm, D_MIXP // tn),
        in_specs=[
            pl.BlockSpec((tm, D_MODEL), lambda i, j: (i, 0)),
            pl.BlockSpec((1, D_MODEL), lambda i, j: (0, 0)),
            pl.BlockSpec((D_MODEL, tn), lambda i, j: (0, j)),
        ],
        out_specs=[
            pl.BlockSpec((tm, tn), lambda i, j: (i, j)),
            pl.BlockSpec((tm, D_MODEL), lambda i, j: (i, 0)),
        ],
        out_shape=[
            jax.ShapeDtypeStruct((t, D_MIXP), F32),
            jax.ShapeDtypeStruct((t, D_MODEL), BF16),
        ],
        scratch_shapes=[pltpu.VMEM((tm, D_MODEL), BF16)],
        compiler_params=_cparams(("parallel", "arbitrary")),
        name="in_proj",
    )(x2d, norm_w, w_mix)


def _layernorm(x, w, b):
    mu = jnp.mean(x, axis=-1, keepdims=True)
    xc = x - mu
    var = jnp.mean(xc * xc, axis=-1, keepdims=True)
    return xc * lax.rsqrt(var + EPS) * w + b


def _mix_kernel(qkv_m, qkv_p, qkv_n, xa_m, xa_p, xa_n, xc_m, xc_p, xc_n, xd_m, ab_m,
                pool_w, pool_scale, gconv_w, ab_a, ab_dtb, cdw_w, cdw_b, cln_w, cln_b,
                sln_w, sln_b, sgu_w, sgu_bias,
                ma_ref, mc_ref, md_ref, q_ref, k_ref, v_ref, abg_ref,
                extq, exta, extc, *, ts, seq_len):
    tiles_per_seq = seq_len // ts
    s = pl.program_id(0) % tiles_per_seq
    left_ok = s > 0
    right_ok = s < tiles_per_seq - 1

    def fill(ext, main, prev, nxt, fn):
        ext[0:HALO, :] = jnp.where(left_ok, fn(prev[...]), 0.0)
        ext[HALO:HALO + ts, :] = fn(main[...])
        ext[HALO + ts:HALO + ts + HALO, :] = jnp.where(right_ok, fn(nxt[...]), 0.0)

    def glu(xc):
        return xc[:, :MIX_W] * _sigmoid(xc[:, MIX_W:])

    fill(extq, qkv_m, qkv_p, qkv_n, lambda x: x)
    fill(exta, xa_m, xa_p, xa_n, lambda x: x)
    fill(extc, xc_m, xc_p, xc_n, glu)

    pos = s * ts + lax.broadcasted_iota(jnp.int32, (ts, 1), 0)
    for g, win in enumerate(POOL_WINDOWS):
        half = win // 2
        cols = slice(g * GROUP_DIM, (g + 1) * GROUP_DIM)
        acc = exta[pl.ds(HALO - half, ts), cols]
        for d in range(-half + 1, half):
            acc = acc + exta[pl.ds(HALO + d, ts), cols]
        lo = jnp.maximum(pos - half, 0)
        hi = jnp.minimum(pos - half + win, seq_len)
        cnt = (hi - lo).astype(F32)
        xg = exta[pl.ds(HALO, ts), cols]
        y = _bdot(acc / cnt - xg, pool_w[g])
        ma_ref[:, cols] = (y * pool_scale[:, cols]).astype(BF16)

    pad_l = (CONF_WIDTH - 1) // 2
    acc = extc[pl.ds(HALO - pad_l, ts), :] * cdw_w[0:1, :]
    for kk in range(1, CONF_WIDTH):
        acc = acc + extc[pl.ds(HALO - pad_l + kk, ts), :] * cdw_w[kk:kk + 1, :]
    hc = _layernorm(acc + cdw_b[...], cln_w[...], cln_b[...])
    mc_ref[...] = (hc * _sigmoid(hc)).astype(BF16)

    xd = xd_m[...]
    gd = jax.nn.gelu(xd, approximate=True)
    u = gd[:, :MIX_W]
    vn = _layernorm(gd[:, MIX_W:], sln_w[...], sln_b[...]).astype(BF16)
    for c in range(ts // SGU_CHUNK):
        rows = slice(c * SGU_CHUNK, (c + 1) * SGU_CHUNK)
        for g in range(N_GROUPS):
            cols = slice(g * GROUP_DIM, (g + 1) * GROUP_DIM)
            sv = jnp.dot(sgu_w[g], vn[rows, cols], preferred_element_type=F32) + sgu_bias[:, cols]
            md_ref[rows, cols] = (u[rows, cols] * sv).astype(BF16)

    gpad_l = (GDN_CONV - 1) // 2
    acc = extq[pl.ds(HALO - gpad_l, ts), :] * gconv_w[0:1, :]
    for kk in range(1, GDN_CONV):
        acc = acc + extq[pl.ds(HALO - gpad_l + kk, ts), :] * gconv_w[kk:kk + 1, :]
    qkv = acc * _sigmoid(acc)
    for hd in range(GDN_HEADS):
        cols = slice(hd * GDN_DK, (hd + 1) * GDN_DK)
        qh = qkv[:, hd * GDN_DK:(hd + 1) * GDN_DK]
        kh = qkv[:, MIX_W + hd * GDN_DK:MIX_W + (hd + 1) * GDN_DK]
        q_ref[:, cols] = qh * lax.rsqrt(jnp.sum(qh * qh, axis=-1, keepdims=True) + EPS) * (GDN_DK ** -0.5)
        k_ref[:, cols] = kh * lax.rsqrt(jnp.sum(kh * kh, axis=-1, keepdims=True) + EPS)
    v_ref[...] = qkv[:, 2 * MIX_W:]

    ab = ab_m[...]
    col = lax.broadcasted_iota(jnp.int32, ab.shape, 1)
    xs = ab + ab_dtb[...]
    softplus = jnp.maximum(xs, 0.0) + jnp.log(1.0 + jnp.exp(-jnp.abs(xs)))
    abg_ref[...] = jnp.where(col < 2 * GDN_HEADS, _sigmoid(ab), ab_a[...] * softplus)


def _mix(p, lw, seq_len, ts=256):
    t = p.shape[0]
    hb = ts // HALO
    n_hb = t // HALO

    def main(width, off):
        return pl.BlockSpec((ts, width), lambda i: (i, off // width))

    def prev(width, off):
        return pl.BlockSpec((HALO, width), lambda i: (jnp.maximum(i * hb - 1, 0), off // width))

    def nxt(width, off):
        return pl.BlockSpec((HALO, width), lambda i: (jnp.minimum((i + 1) * hb, n_hb - 1), off // width))

    def full(a):
        nd = a.ndim
        return pl.BlockSpec(a.shape, lambda i: (0,) * nd)

    params = (lw["pool_w"], lw["pool_scale"], lw["gconv_w"], lw["ab_a"], lw["ab_dtb"], lw["cdw_w"],
              lw["cdw_b"], lw["cln_w"], lw["cln_b"], lw["sln_w"], lw["sln_b"], lw["sgu_w"], lw["sgu_bias"])
    in_specs = [
        main(3 * MIX_W, OFF_QKV), prev(3 * MIX_W, OFF_QKV), nxt(3 * MIX_W, OFF_QKV),
        main(MIX_W, OFF_XA), prev(MIX_W, OFF_XA), nxt(MIX_W, OFF_XA),
        main(2 * MIX_W, OFF_XC), prev(2 * MIX_W, OFF_XC), nxt(2 * MIX_W, OFF_XC),
        main(2 * MIX_W, OFF_XD),
        main(LANE, OFF_AB),
    ] + [full(a) for a in params]
    tok = lambda w: pl.BlockSpec((ts, w), lambda i: (i, 0))
    out_specs = [tok(MIX_W)] * 6 + [tok(LANE)]
    out_shape = ([jax.ShapeDtypeStruct((t, MIX_W), BF16)] * 3 + [jax.ShapeDtypeStruct((t, MIX_W), F32)] * 3
                 + [jax.ShapeDtypeStruct((t, LANE), F32)])
    return pl.pallas_call(
        functools.partial(_mix_kernel, ts=ts, seq_len=seq_len),
        grid=(t // ts,),
        in_specs=in_specs,
        out_specs=out_specs,
        out_shape=out_shape,
        scratch_shapes=[
            pltpu.VMEM((ts + 2 * HALO, 3 * MIX_W), F32),
            pltpu.VMEM((ts + 2 * HALO, MIX_W), F32),
            pltpu.VMEM((ts + 2 * HALO, MIX_W), F32),
        ],
        compiler_params=_cparams(("parallel",)),
        name="mix",
    )(p, p, p, p, p, p, p, p, p, p, p, *params)


def _gdn_kernel(*refs, reverse, blocks_per_seq):
    if reverse:
        q_ref, k_ref, v_ref, ab_ref, of_ref, z_ref, nw_ref, o_ref, s_ref = refs
    else:
        q_ref, k_ref, v_ref, ab_ref, o_ref, s_ref = refs

    @pl.when(pl.program_id(1) == 0)
    def _():
        s_ref[...] = jnp.zeros_like(s_ref)

    n = GDN_BLOCK
    row = lax.broadcasted_iota(jnp.int32, (n, n), 0)
    col = lax.broadcasted_iota(jnp.int32, (n, n), 1)
    same = (row // GDN_CHUNK) == (col // GDN_CHUNK)
    if reverse:
        tri = same & (col >= row)
        strict = same & (col > row)
    else:
        tri = same & (col <= row)
        strict = same & (col < row)
    eye = (row == col).astype(F32)

    ab = ab_ref[...]
    gc_all = jnp.dot(tri.astype(F32), ab, precision=lax.Precision.HIGHEST, preferred_element_type=F32)
    gc_all_t = gc_all.T
    dirn = 1 if reverse else 0
    rowv = lax.broadcasted_iota(jnp.int32, (n, 1), 0)
    last = (0, GDN_CHUNK) if reverse else (GDN_CHUNK - 1, n - 1)
    order = (1, 0) if reverse else (0, 1)

    for hd in range(GDN_HEADS):
        cols = slice(hd * GDN_DK, (hd + 1) * GDN_DK)
        cb = dirn * GDN_HEADS + hd
        cg = 2 * GDN_HEADS + dirn * GDN_HEADS + hd
        beta = ab[:, cb:cb + 1]
        gcc = gc_all[:, cg:cg + 1]
        gcr = gc_all_t[cg:cg + 1, :]
        decay = jnp.exp(jnp.where(tri, gcc - gcr, -jnp.inf))
        kh = k_ref[:, cols]
        qh = q_ref[:, cols]
        vh = v_ref[:, cols]
        kh16 = kh.astype(BF16)
        dn = (((1,), (1,)), ((), ()))
        gram = lax.dot_general(kh16, kh16, dn, preferred_element_type=F32)
        qkm = lax.dot_general(qh.astype(BF16), kh16, dn, preferred_element_type=F32) * decay
        m = jnp.where(strict, -(beta * gram * decay), 0.0)
        toff = m
        p = m
        for _ in range(5):
            p = jnp.dot(p, p, precision=lax.Precision.HIGHEST, preferred_element_type=F32)
            toff = toff + p + jnp.dot(toff, p, precision=lax.Precision.HIGHEST, preferred_element_type=F32)
        eg = jnp.exp(gcc)
        rhs = jnp.concatenate([vh * beta, kh * (beta * eg)], axis=1)
        uw = rhs + jnp.dot(toff, rhs, precision=lax.Precision.HIGHEST, preferred_element_type=F32)
        u = uw[:, :GDN_DK]
        w = uw[:, GDN_DK:]
        qg = qh * eg
        gl = [gc_all[r:r + 1, cg:cg + 1] for r in last]
        glc = jnp.where(rowv < GDN_CHUNK, gl[0], gl[1])
        kg_t = (kh * jnp.exp(glc - gcc)).T
        state = s_ref[hd]
        outs = [None, None]
        for c in order:
            rows = slice(c * GDN_CHUNK, (c + 1) * GDN_CHUNK)
            ws = _bdot(jnp.concatenate([w[rows], qg[rows]], axis=0), state)
            v_new = u[rows] - ws[:GDN_CHUNK]
            outs[c] = ws[GDN_CHUNK:] + _bdot(qkm[rows, rows], v_new)
            state = state * jnp.exp(gl[c]) + _bdot(kg_t[:, rows], v_new)
        s_ref[hd] = state
        o = jnp.concatenate(outs, axis=0)
        if reverse:
            o = o + of_ref[:, cols]
            o = o * lax.rsqrt(jnp.mean(o * o, axis=-1, keepdims=True) + EPS) * nw_ref[...]
            z = z_ref[:, cols]
            o_ref[:, cols] = (o * (z * _sigmoid(z))).astype(BF16)
        else:
            o_ref[:, cols] = o


def _gdn(q, k, v, abg, seq_len, reverse, o_fwd=None, p=None, norm_w=None):
    t = q.shape[0]
    n_seq = t // seq_len
    bps = seq_len // GDN_BLOCK
    if reverse:
        blk = lambda b, i: (b * bps + bps - 1 - i, 0)
    else:
        blk = lambda b, i: (b * bps + i, 0)
    tok = pl.BlockSpec((GDN_BLOCK, MIX_W), blk)
    in_specs = [tok, tok, tok, pl.BlockSpec((GDN_BLOCK, LANE), blk)]
    args = [q, k, v, abg]
    if reverse:
        zoff = OFF_XZ // MIX_W
        in_specs += [tok,
                     pl.BlockSpec((GDN_BLOCK, MIX_W), lambda b, i: (b * bps + bps - 1 - i, zoff)),
                     pl.BlockSpec((1, GDN_DK), lambda b, i: (0, 0))]
        args += [o_fwd, p, norm_w]
    return pl.pallas_call(
        functools.partial(_gdn_kernel, reverse=reverse, blocks_per_seq=bps),
        grid=(n_seq, bps),
        in_specs=in_specs,
        out_specs=tok,
        out_shape=jax.ShapeDtypeStruct((t, MIX_W), BF16 if reverse else F32),
        scratch_shapes=[pltpu.VMEM((GDN_HEADS, GDN_DK, GDN_DK), F32)],
        compiler_params=_cparams(("arbitrary", "arbitrary")),
        name="gdn_bwd" if reverse else "gdn_fwd",
    )(*args)


def _merge_kernel(h_ref, ma, mb, mc, md, wg0, wg1, wg2, wg3, pa, pb, pc, pd, o_ref):
    h = h_ref[...]
    acc = None
    for m_ref, wg, pr in ((ma, wg0, pa), (mb, wg1, pb), (mc, wg2, pc), (md, wg3, pd)):
        gate = _sigmoid(jnp.dot(h, wg[...], preferred_element_type=F32))
        y = jnp.dot(m_ref[...], pr[...], preferred_element_type=F32)
        acc = gate * y if acc is None else acc + gate * y
    o_ref[...] = acc.astype(BF16)


def _merge(h, ms, w_gate, projs, tm=512, tn=512):
    t = h.shape[0]
    nj = D_MODEL // tn
    tokm = pl.BlockSpec((tm, MIX_W), lambda i, j: (i, 0))
    in_specs = [pl.BlockSpec((tm, D_MODEL), lambda i, j: (i, 0))] + [tokm] * 4
    in_specs += [pl.BlockSpec((D_MODEL, tn), functools.partial(lambda i, j, b: (0, b * nj + j), b=b))
                 for b in range(N_BRANCH)]
    in_specs += [pl.BlockSpec((MIX_W, tn), lambda i, j: (0, j))] * 4
    return pl.pallas_call(
        _merge_kernel,
        grid=(t // tm, nj),
        in_specs=in_specs,
        out_specs=pl.BlockSpec((tm, tn), lambda i, j: (i, j)),
        out_shape=jax.ShapeDtypeStruct((t, D_MODEL), BF16),
        compiler_params=_cparams(("parallel", "arbitrary")),
        name="merge",
    )(h, *ms, w_gate, w_gate, w_gate, w_gate, *projs)


def _outproj_kernel(x_ref, m_ref, w_ref, nw_ref, x1_ref, h2_ref):
    x1 = x_ref[...] + jnp.dot(m_ref[...], w_ref[...], preferred_element_type=F32)
    x1_ref[...] = x1
    ms = jnp.mean(x1 * x1, axis=-1, keepdims=True)
    h2_ref[...] = ((x1 * lax.rsqrt(ms + EPS)) * nw_ref[...]).astype(BF16)


def _out_proj(x2d, merged, w_out, norm_w, tm=256):
    t = x2d.shape[0]
    tok = pl.BlockSpec((tm, D_MODEL), lambda i: (i, 0))
    return pl.pallas_call(
        _outproj_kernel,
        grid=(t // tm,),
        in_specs=[tok, tok, pl.BlockSpec((D_MODEL, D_MODEL), lambda i: (0, 0)),
                  pl.BlockSpec((1, D_MODEL), lambda i: (0, 0))],
        out_specs=[tok, tok],
        out_shape=[jax.ShapeDtypeStruct((t, D_MODEL), F32), jax.ShapeDtypeStruct((t, D_MODEL), BF16)],
        compiler_params=_cparams(("parallel",)),
        name="out_proj",
    )(x2d, merged, w_out, norm_w)


def _mlp_kernel(x1_ref, h2_ref, w1_ref, w2_ref, nw_ref, o_ref, acc_ref, *, final):
    f = pl.program_id(1)

    @pl.when(f == 0)
    def _():
        acc_ref[...] = x1_ref[...]

    a = jnp.maximum(jnp.dot(h2_ref[...], w1_ref[...], preferred_element_type=F32), 0.0)
    acc_ref[...] += jnp.dot((a * a).astype(BF16), w2_ref[...], preferred_element_type=F32)

    @pl.when(f == pl.num_programs(1) - 1)
    def _():
        x2 = acc_ref[...]
        if final:
            ms = jnp.mean(x2 * x2, axis=-1, keepdims=True)
            x2 = (x2 * lax.rsqrt(ms + EPS)) * nw_ref[...]
        o_ref[...] = x2


def _mlp(x1, h2, w1, w2, norm_w, final, tm=512, tf=512):
    t = x1.shape[0]
    tok = pl.BlockSpec((tm, D_MODEL), lambda i, f: (i, 0))
    return pl.pallas_call(
        functools.partial(_mlp_kernel, final=final),
        grid=(t // tm, D_FF // tf),
        in_specs=[tok, tok,
                  pl.BlockSpec((D_MODEL, tf), lambda i, f: (0, f)),
                  pl.BlockSpec((tf, D_MODEL), lambda i, f: (f, 0)),
                  pl.BlockSpec((1, D_MODEL), lambda i, f: (0, 0))],
        out_specs=tok,
        out_shape=jax.ShapeDtypeStruct((t, D_MODEL), F32),
        scratch_shapes=[pltpu.VMEM((tm, D_MODEL), F32)],
        compiler_params=_cparams(("parallel", "arbitrary")),
        name="mlp",
    )(x1, h2, w1, w2, norm_w)


def _prep_layer(l, norm_mix_w, w_in, pool_w, pool_scale, pool_proj, gdn_conv_w, gdn_a_log, gdn_dt_bias,
                gdn_norm_w, gdn_proj, conf_dw_w, conf_dw_b, conf_ln_w, conf_ln_b, conf_proj, sgu_ln_w,
                sgu_ln_b, sgu_w, sgu_b, sgu_proj, w_out, norm_mlp_w, mlp_w1, mlp_w2):
    w = w_in[l]
    o_a, o_qkv, o_z, o_ab, o_c, o_d, o_g = 0, MIX_W, 4 * MIX_W, 5 * MIX_W, 5 * MIX_W + 16, 7 * MIX_W + 16, 9 * MIX_W + 16
    w_mix = jnp.concatenate([
        w[:, o_qkv:o_z], w[:, o_a:o_qkv], w[:, o_c:o_d], w[:, o_d:o_g], w[:, o_z:o_ab], w[:, o_ab:o_c],
        jnp.zeros((D_MODEL, D_MIXP - OFF_AB - 16), F32)], axis=1).astype(BF16)
    row = lambda a: a.reshape(1, -1).astype(F32)
    pad_row = lambda a: jnp.zeros((1, LANE), F32).at[0, 2 * GDN_HEADS:4 * GDN_HEADS].set(a.reshape(-1))
    return dict(
        norm_mix_w=row(norm_mix_w[l]),
        w_mix=w_mix,
        w_gate=w[:, o_g:].astype(BF16),
        pool_w=pool_w[l].astype(BF16),
        pool_scale=row(pool_scale[l]),
        gconv_w=gdn_conv_w[l].astype(F32),
        ab_a=pad_row(-jnp.exp(gdn_a_log[l].astype(F32))),
        ab_dtb=pad_row(gdn_dt_bias[l].astype(F32)),
        gdn_norm_w=row(gdn_norm_w[l]),
        cdw_w=conf_dw_w[l].astype(F32),
        cdw_b=row(conf_dw_b[l]),
        cln_w=row(conf_ln_w[l]),
        cln_b=row(conf_ln_b[l]),
        sln_w=row(sgu_ln_w[l]),
        sln_b=row(sgu_ln_b[l]),
        sgu_w=sgu_w[l].astype(BF16),
        sgu_bias=jnp.repeat(sgu_b[l].astype(F32).T, GROUP_DIM, axis=1),
        projs=tuple(pr[l].astype(BF16) for pr in (pool_proj, gdn_proj, conf_proj, sgu_proj)),
        w_out=w_out[l].astype(BF16),
        norm_mlp_w=row(norm_mlp_w[l]),
        w1=mlp_w1[l].astype(BF16),
        w2=mlp_w2[l].astype(BF16),
    )


def _encoder(x, layers, norm_final_w):
    b, s, _ = x.shape
    x2d = x.reshape(b * s, D_MODEL)
    nf = norm_final_w.reshape(1, -1).astype(F32)
    for li, lw in enumerate(layers):
        p, h = _in_proj(x2d, lw["norm_mix_w"], lw["w_mix"])
        m_a, m_c, m_d, q, k, v, abg = _mix(p, lw, s)
        o_f = _gdn(q, k, v, abg, s, reverse=False)
        m_b = _gdn(q, k, v, abg, s, reverse=True, o_fwd=o_f, p=p, norm_w=lw["gdn_norm_w"])
        merged = _merge(h, (m_a, m_b, m_c, m_d), lw["w_gate"], lw["projs"])
        x1, h2 = _out_proj(x2d, merged, lw["w_out"], lw["norm_mlp_w"])
        x2d = _mlp(x1, h2, lw["w1"], lw["w2"], nf, final=(li == len(layers) - 1))
    return x2d.reshape(b, s, D_MODEL)


def kernel(x_prompt, x_sample, norm_mix_w, w_in, pool_w, pool_scale, pool_proj, gdn_conv_w, gdn_a_log,
           gdn_dt_bias, gdn_norm_w, gdn_proj, conf_dw_w, conf_dw_b, conf_ln_w, conf_ln_b, conf_proj,
           sgu_ln_w, sgu_ln_b, sgu_w, sgu_b, sgu_proj, w_out, norm_mlp_w, mlp_w1, mlp_w2, norm_final_w):
    depth = w_in.shape[0]
    layers = [_prep_layer(l, norm_mix_w, w_in, pool_w, pool_scale, pool_proj, gdn_conv_w, gdn_a_log,
                          gdn_dt_bias, gdn_norm_w, gdn_proj, conf_dw_w, conf_dw_b, conf_ln_w, conf_ln_b,
                          conf_proj, sgu_ln_w, sgu_ln_b, sgu_w, sgu_b, sgu_proj, w_out, norm_mlp_w,
                          mlp_w1, mlp_w2) for l in range(depth)]
    y_prompt = _encoder(x_prompt, layers, norm_final_w)
    y_sample = _encoder(x_sample, layers, norm_final_w)
    return (y_prompt, y_sample)
```

```python
import functools

import jax
import jax.numpy as jnp
from jax import lax
from jax.experimental import pallas as pl
from jax.experimental.pallas import tpu as pltpu

F32 = jnp.float32
BF16 = jnp.bfloat16

D_MODEL = 2048
N_GROUPS = 4
GROUP_DIM = 128
MIX_W = N_GROUPS * GROUP_DIM
POOL_WINDOWS = (2, 4, 8, 16)
GDN_HEADS = 4
GDN_DK = 128
GDN_CONV = 4
GDN_CHUNK = 64
CONF_WIDTH = 31
SGU_CHUNK = 128
D_FF = 4 * D_MODEL
N_BRANCH = 4
EPS = 1e-6

LANE = 128
HALO = 16
GDN_BLOCK = 2 * GDN_CHUNK

OFF_QKV = 0
OFF_XA = 3 * MIX_W
OFF_XC = 4 * MIX_W
OFF_XD = 6 * MIX_W
OFF_XZ = 8 * MIX_W
OFF_AB = 9 * MIX_W
D_MIXP = 10 * MIX_W

VMEM_LIMIT = 60 * 1024 * 1024


def _cparams(sem):
    return pltpu.CompilerParams(dimension_semantics=sem, vmem_limit_bytes=VMEM_LIMIT)


def _sigmoid(x):
    return 1.0 / (1.0 + jnp.exp(-x))


def _bdot(a, b):
    return jnp.dot(a.astype(BF16), b.astype(BF16), preferred_element_type=F32)


def _inproj_kernel(x_ref, nw_ref, w_ref, p_ref, h_ref, hs_ref, *, row_chunk):
    j = pl.program_id(1)

    @pl.when(j == 0)
    def _():
        for r in range(x_ref.shape[0] // row_chunk):
            rows = pl.ds(r * row_chunk, row_chunk)
            x = x_ref[rows, :]
            ms = jnp.mean(x * x, axis=-1, keepdims=True)
            h = ((x * lax.rsqrt(ms + EPS)) * nw_ref[...]).astype(BF16)
            hs_ref[rows, :] = h
            h_ref[rows, :] = h
            p_ref[rows, :] = jnp.dot(h, w_ref[...], preferred_element_type=F32)

    @pl.when(j != 0)
    def _():
        p_ref[...] = jnp.dot(hs_ref[...], w_ref[...], preferred_element_type=F32)


def _in_proj(x2d, norm_w, w_mix, tm=1024, tn=1024, row_chunk=256):
    t = x2d.shape[0]
    return pl.pallas_call(
        functools.partial(_inproj_kernel, row_chunk=row_chunk),
        grid=(t // tm, D_MIXP // tn),
        in_specs=[
            pl.BlockSpec((tm, D_MODEL), lambda i, j: (i, 0)),
            pl.BlockSpec((1, D_MODEL), lambda i, j: (0, 0)),
            pl.BlockSpec((D_MODEL, tn), lambda i, j: (0, j)),
        ],
        out_specs=[
            pl.BlockSpec((tm, tn), lambda i, j: (i, j)),
            pl.BlockSpec((tm, D_MODEL), lambda i, j: (i, 0)),
        ],
        out_shape=[
            jax.ShapeDtypeStruct((t, D_MIXP), F32),
            jax.ShapeDtypeStruct((t, D_MODEL), BF16),
        ],
        scratch_shapes=[pltpu.VMEM((tm, D_MODEL), BF16)],
        compiler_params=_cparams(("parallel", "arbitrary")),
        name="in_proj",
    )(x2d, norm_w, w_mix)


def _layernorm(x, w, b):
    mu = jnp.mean(x, axis=-1, keepdims=True)
    xc = x - mu
    var = jnp.mean(xc * xc, axis=-1, keepdims=True)
    return xc * lax.rsqrt(var + EPS) * w + b


def _mix_kernel(qkv_m, qkv_p, qkv_n, xa_m, xa_p, xa_n, xc_m, xc_p, xc_n, xd_m, ab_m,
                pool_w, pool_scale, gconv_w, ab_a, ab_dtb, cdw_w, cdw_b, cln_w, cln_b,
                sln_w, sln_b, sgu_w, sgu_bias,
                ma_ref, mc_ref, md_ref, q_ref, k_ref, v_ref, abg_ref,
                extq, exta, extc, *, ts, seq_len):
    tiles_per_seq = seq_len // ts
    s = pl.program_id(0) % tiles_per_seq
    left_ok = s > 0
    right_ok = s < tiles_per_seq - 1

    def fill(ext, main, prev, nxt, fn):
        ext[0:HALO, :] = jnp.where(left_ok, fn(prev[...]), 0.0)
        ext[HALO:HALO + ts, :] = fn(main[...])
        ext[HALO + ts:HALO + ts + HALO, :] = jnp.where(right_ok, fn(nxt[...]), 0.0)

    def glu(xc):
        return xc[:, :MIX_W] * _sigmoid(xc[:, MIX_W:])

    fill(extq, qkv_m, qkv_p, qkv_n, lambda x: x)
    fill(exta, xa_m, xa_p, xa_n, lambda x: x)
    fill(extc, xc_m, xc_p, xc_n, glu)

    pos = s * ts + lax.broadcasted_iota(jnp.int32, (ts, 1), 0)
    for g, win in enumerate(POOL_WINDOWS):
        half = win // 2
        cols = slice(g * GROUP_DIM, (g + 1) * GROUP_DIM)
        acc = exta[pl.ds(HALO - half, ts), cols]
        for d in range(-half + 1, half):
            acc = acc + exta[pl.ds(HALO + d, ts), cols]
        lo = jnp.maximum(pos - half, 0)
        hi = jnp.minimum(pos - half + win, seq_len)
        cnt = (hi - lo).astype(F32)
        xg = exta[pl.ds(HALO, ts), cols]
        y = _bdot(acc / cnt - xg, pool_w[g])
        ma_ref[:, cols] = (y * pool_scale[:, cols]).astype(BF16)

    pad_l = (CONF_WIDTH - 1) // 2
    acc = extc[pl.ds(HALO - pad_l, ts), :] * cdw_w[0:1, :]
    for kk in range(1, CONF_WIDTH):
        acc = acc + extc[pl.ds(HALO - pad_l + kk, ts), :] * cdw_w[kk:kk + 1, :]
    hc = _layernorm(acc + cdw_b[...], cln_w[...], cln_b[...])
    mc_ref[...] = (hc * _sigmoid(hc)).astype(BF16)

    xd = xd_m[...]
    gd = jax.nn.gelu(xd, approximate=True)
    u = gd[:, :MIX_W]
    vn = _layernorm(gd[:, MIX_W:], sln_w[...], sln_b[...]).astype(BF16)
    for c in range(ts // SGU_CHUNK):
        rows = slice(c * SGU_CHUNK, (c + 1) * SGU_CHUNK)
        for g in range(N_GROUPS):
            cols = slice(g * GROUP_DIM, (g + 1) * GROUP_DIM)
            sv = jnp.dot(sgu_w[g], vn[rows, cols], preferred_element_type=F32) + sgu_bias[:, cols]
            md_ref[rows, cols] = (u[rows, cols] * sv).astype(BF16)

    gpad_l = (GDN_CONV - 1) // 2
    acc = extq[pl.ds(HALO - gpad_l, ts), :] * gconv_w[0:1, :]
    for kk in range(1, GDN_CONV):
        acc = acc + extq[pl.ds(HALO - gpad_l + kk, ts), :] * gconv_w[kk:kk + 1, :]
    qkv = acc * _sigmoid(acc)
    for hd in range(GDN_HEADS):
        cols = slice(hd * GDN_DK, (hd + 1) * GDN_DK)
        qh = qkv[:, hd * GDN_DK:(hd + 1) * GDN_DK]
        kh = qkv[:, MIX_W + hd * GDN_DK:MIX_W + (hd + 1) * GDN_DK]
        q_ref[:, cols] = qh * lax.rsqrt(jnp.sum(qh * qh, axis=-1, keepdims=True) + EPS) * (GDN_DK ** -0.5)
        k_ref[:, cols] = kh * lax.rsqrt(jnp.sum(kh * kh, axis=-1, keepdims=True) + EPS)
    v_ref[...] = qkv[:, 2 * MIX_W:]

    ab = ab_m[...]
    col = lax.broadcasted_iota(jnp.int32, ab.shape, 1)
    xs = ab + ab_dtb[...]
    softplus = jnp.maximum(xs, 0.0) + jnp.log(1.0 + jnp.exp(-jnp.abs(xs)))
    abg_ref[...] = jnp.where(col < 2 * GDN_HEADS, _sigmoid(ab), ab_a[...] * softplus)


def _mix(p, lw, seq_len, ts=256):
    t = p.shape[0]
    hb = ts // HALO
    n_hb = t // HALO

    def main(width, off):
        return pl.BlockSpec((ts, width), lambda i: (i, off // width))

    def prev(width, off):
        return pl.BlockSpec((HALO, width), lambda i: (jnp.maximum(i * hb - 1, 0), off // width))

    def nxt(width, off):
        return pl.BlockSpec((HALO, width), lambda i: (jnp.minimum((i + 1) * hb, n_hb - 1), off // width))

    def full(a):
        nd = a.ndim
        return pl.BlockSpec(a.shape, lambda i: (0,) * nd)

    params = (lw["pool_w"], lw["pool_scale"], lw["gconv_w"], lw["ab_a"], lw["ab_dtb"], lw["cdw_w"],
              lw["cdw_b"], lw["cln_w"], lw["cln_b"], lw["sln_w"], lw["sln_b"], lw["sgu_w"], lw["sgu_bias"])
    in_specs = [
        main(3 * MIX_W, OFF_QKV), prev(3 * MIX_W, OFF_QKV), nxt(3 * MIX_W, OFF_QKV),
        main(MIX_W, OFF_XA), prev(MIX_W, OFF_XA), nxt(MIX_W, OFF_XA),
        main(2 * MIX_W, OFF_XC), prev(2 * MIX_W, OFF_XC), nxt(2 * MIX_W, OFF_XC),
        main(2 * MIX_W, OFF_XD),
        main(LANE, OFF_AB),
    ] + [full(a) for a in params]
    tok = lambda w: pl.BlockSpec((ts, w), lambda i: (i, 0))
    out_specs = [tok(MIX_W)] * 6 + [tok(LANE)]
    out_shape = ([jax.ShapeDtypeStruct((t, MIX_W), BF16)] * 3 + [jax.ShapeDtypeStruct((t, MIX_W), F32)] * 3
                 + [jax.ShapeDtypeStruct((t, LANE), F32)])
    return pl.pallas_call(
        functools.partial(_mix_kernel, ts=ts, seq_len=seq_len),
        grid=(t // ts,),
        in_specs=in_specs,
        out_specs=out_specs,
        out_shape=out_shape,
        scratch_shapes=[
            pltpu.VMEM((ts + 2 * HALO, 3 * MIX_W), F32),
            pltpu.VMEM((ts + 2 * HALO, MIX_W), F32),
            pltpu.VMEM((ts + 2 * HALO, MIX_W), F32),
        ],
        compiler_params=_cparams(("parallel",)),
        name="mix",
    )(p, p, p, p, p, p, p, p, p, p, p, *params)


def _gdn_kernel(*refs, reverse, blocks_per_seq):
    if reverse:
        q_ref, k_ref, v_ref, ab_ref, of_ref, z_ref, nw_ref, o_ref, s_ref = refs
    else:
        q_ref, k_ref, v_ref, ab_ref, o_ref, s_ref = refs

    @pl.when(pl.program_id(1) == 0)
    def _():
        s_ref[...] = jnp.zeros_like(s_ref)

    n = GDN_BLOCK
    row = lax.broadcasted_iota(jnp.int32, (n, n), 0)
    col = lax.broadcasted_iota(jnp.int32, (n, n), 1)
    same = (row // GDN_CHUNK) == (col // GDN_CHUNK)
    if reverse:
        tri = same & (col >= row)
        strict = same & (col > row)
    else:
        tri = same & (col <= row)
        strict = same & (col < row)
    eye = (row == col).astype(F32)

    ab = ab_ref[...]
    gc_all = jnp.dot(tri.astype(F32), ab, precision=lax.Precision.HIGHEST, preferred_element_type=F32)
    gc_all_t = gc_all.T
    dirn = 1 if reverse else 0
    rowv = lax.broadcasted_iota(jnp.int32, (n, 1), 0)
    last = (0, GDN_CHUNK) if reverse else (GDN_CHUNK - 1, n - 1)
    order = (1, 0) if reverse else (0, 1)

    heads = range(GDN_HEADS)
    hcols = [slice(hd * GDN_DK, (hd + 1) * GDN_DK) for hd in heads]
    cgs = [2 * GDN_HEADS + dirn * GDN_HEADS + hd for hd in heads]
    beta = [ab[:, dirn * GDN_HEADS + hd:dirn * GDN_HEADS + hd + 1] for hd in heads]
    gcc = [gc_all[:, cg:cg + 1] for cg in cgs]
    decay = [jnp.exp(jnp.where(tri, gcc[hd] - gc_all_t[cgs[hd]:cgs[hd] + 1, :], -jnp.inf)) for hd in heads]
    kh = [k_ref[:, c] for c in hcols]
    qh = [q_ref[:, c] for c in hcols]
    vh = [v_ref[:, c] for c in hcols]
    kh16 = [x.astype(BF16) for x in kh]
    dn = (((1,), (1,)), ((), ()))
    gram = [lax.dot_general(kh16[hd], kh16[hd], dn, preferred_element_type=F32) for hd in heads]
    qkm = [lax.dot_general(qh[hd].astype(BF16), kh16[hd], dn, preferred_element_type=F32) * decay[hd]
           for hd in heads]
    toff = [jnp.where(strict, -(beta[hd] * gram[hd] * decay[hd]), 0.0) for hd in heads]
    p = toff
    for _ in range(5):
        p = [_bdot(p[hd], p[hd]) for hd in heads]
        toff = [toff[hd] + p[hd] + _bdot(toff[hd], p[hd]) for hd in heads]
    eg = [jnp.exp(g) for g in gcc]
    rhs = [jnp.concatenate([vh[hd] * beta[hd], kh[hd] * (beta[hd] * eg[hd])], axis=1) for hd in heads]
    uw = [rhs[hd] + _bdot(toff[hd], rhs[hd]) for hd in heads]
    qg = [qh[hd] * eg[hd] for hd in heads]
    gl = [[gc_all[r:r + 1, cg:cg + 1] for r in last] for cg in cgs]
    kg_t = [(kh[hd] * jnp.exp(jnp.where(rowv < GDN_CHUNK, gl[hd][0], gl[hd][1]) - gcc[hd])).T for hd in heads]
    state = [s_ref[hd] for hd in heads]
    outs = [[None, None] for _ in heads]
    for c in order:
        rows = slice(c * GDN_CHUNK, (c + 1) * GDN_CHUNK)
        ws = [_bdot(jnp.concatenate([uw[hd][rows, GDN_DK:], qg[hd][rows]], axis=0), state[hd]) for hd in heads]
        v_new = [uw[hd][rows, :GDN_DK] - ws[hd][:GDN_CHUNK] for hd in heads]
        for hd in heads:
            outs[hd][c] = ws[hd][GDN_CHUNK:] + _bdot(qkm[hd][rows, rows], v_new[hd])
        state = [state[hd] * jnp.exp(gl[hd][c]) + _bdot(kg_t[hd][:, rows], v_new[hd]) for hd in heads]
    for hd in heads:
        s_ref[hd] = state[hd]
        o = jnp.concatenate(outs[hd], axis=0)
        if reverse:
            o = o + of_ref[:, hcols[hd]]
            o = o * lax.rsqrt(jnp.mean(o * o, axis=-1, keepdims=True) + EPS) * nw_ref[...]
            z = z_ref[:, hcols[hd]]
            o_ref[:, hcols[hd]] = (o * (z * _sigmoid(z))).astype(BF16)
        else:
            o_ref[:, hcols[hd]] = o


def _gdn(q, k, v, abg, seq_len, reverse, o_fwd=None, p=None, norm_w=None):
    t = q.shape[0]
    n_seq = t // seq_len
    bps = seq_len // GDN_BLOCK
    if reverse:
        blk = lambda b, i: (b * bps + bps - 1 - i, 0)
    else:
        blk = lambda b, i: (b * bps + i, 0)
    tok = pl.BlockSpec((GDN_BLOCK, MIX_W), blk)
    in_specs = [tok, tok, tok, pl.BlockSpec((GDN_BLOCK, LANE), blk)]
    args = [q, k, v, abg]
    if reverse:
        zoff = OFF_XZ // MIX_W
        in_specs += [tok,
                     pl.BlockSpec((GDN_BLOCK, MIX_W), lambda b, i: (b * bps + bps - 1 - i, zoff)),
                     pl.BlockSpec((1, GDN_DK), lambda b, i: (0, 0))]
        args += [o_fwd, p, norm_w]
    return pl.pallas_call(
        functools.partial(_gdn_kernel, reverse=reverse, blocks_per_seq=bps),
        grid=(n_seq, bps),
        in_specs=in_specs,
        out_specs=tok,
        out_shape=jax.ShapeDtypeStruct((t, MIX_W), BF16 if reverse else F32),
        scratch_shapes=[pltpu.VMEM((GDN_HEADS, GDN_DK, GDN_DK), F32)],
        compiler_params=_cparams(("arbitrary", "arbitrary")),
        name="gdn_bwd" if reverse else "gdn_fwd",
    )(*args)


def _merge_kernel(h_ref, ma, mb, mc, md, wg0, wg1, wg2, wg3, pa, pb, pc, pd, o_ref):
    h = h_ref[...]
    acc = None
    for m_ref, wg, pr in ((ma, wg0, pa), (mb, wg1, pb), (mc, wg2, pc), (md, wg3, pd)):
        gate = _sigmoid(jnp.dot(h, wg[...], preferred_element_type=F32))
        y = jnp.dot(m_ref[...], pr[...], preferred_element_type=F32)
        acc = gate * y if acc is None else acc + gate * y
    o_ref[...] = acc.astype(BF16)


def _merge(h, ms, w_gate, projs, tm=512, tn=512):
    t = h.shape[0]
    nj = D_MODEL // tn
    tokm = pl.BlockSpec((tm, MIX_W), lambda i, j: (i, 0))
    in_specs = [pl.BlockSpec((tm, D_MODEL), lambda i, j: (i, 0))] + [tokm] * 4
    in_specs += [pl.BlockSpec((D_MODEL, tn), functools.partial(lambda i, j, b: (0, b * nj + j), b=b))
                 for b in range(N_BRANCH)]
    in_specs += [pl.BlockSpec((MIX_W, tn), lambda i, j: (0, j))] * 4
    return pl.pallas_call(
        _merge_kernel,
        grid=(t // tm, nj),
        in_specs=in_specs,
        out_specs=pl.BlockSpec((tm, tn), lambda i, j: (i, j)),
        out_shape=jax.ShapeDtypeStruct((t, D_MODEL), BF16),
        compiler_params=_cparams(("parallel", "arbitrary")),
        name="merge",
    )(h, *ms, w_gate, w_gate, w_gate, w_gate, *projs)


def _outproj_kernel(x_ref, m_ref, w_ref, nw_ref, x1_ref, h2_ref):
    x1 = x_ref[...] + jnp.dot(m_ref[...], w_ref[...], preferred_element_type=F32)
    x1_ref[...] = x1
    ms = jnp.mean(x1 * x1, axis=-1, keepdims=True)
    h2_ref[...] = ((x1 * lax.rsqrt(ms + EPS)) * nw_ref[...]).astype(BF16)


def _out_proj(x2d, merged, w_out, norm_w, tm=512):
    t = x2d.shape[0]
    tok = pl.BlockSpec((tm, D_MODEL), lambda i: (i, 0))
    return pl.pallas_call(
        _outproj_kernel,
        grid=(t // tm,),
        in_specs=[tok, tok, pl.BlockSpec((D_MODEL, D_MODEL), lambda i: (0, 0)),
                  pl.BlockSpec((1, D_MODEL), lambda i: (0, 0))],
        out_specs=[tok, tok],
        out_shape=[jax.ShapeDtypeStruct((t, D_MODEL), F32), jax.ShapeDtypeStruct((t, D_MODEL), BF16)],
        compiler_params=_cparams(("parallel",)),
        name="out_proj",
    )(x2d, merged, w_out, norm_w)


def _mlp_kernel(x1_ref, h2_ref, w1_ref, w2_ref, nw_ref, o_ref, acc_ref, *, final):
    f = pl.program_id(1)

    @pl.when(f == 0)
    def _():
        acc_ref[...] = x1_ref[...]

    a = jnp.maximum(jnp.dot(h2_ref[...], w1_ref[...], preferred_element_type=F32), 0.0)
    acc_ref[...] += jnp.dot((a * a).astype(BF16), w2_ref[...], preferred_element_type=F32)

    @pl.when(f == pl.num_programs(1) - 1)
    def _():
        x2 = acc_ref[...]
        if final:
            ms = jnp.mean(x2 * x2, axis=-1, keepdims=True)
            x2 = (x2 * lax.rsqrt(ms + EPS)) * nw_ref[...]
        o_ref[...] = x2


def _mlp(x1, h2, w1, w2, norm_w, final, tm=512, tf=1024):
    t = x1.shape[0]
    tok = pl.BlockSpec((tm, D_MODEL), lambda i, f: (i, 0))
    return pl.pallas_call(
        functools.partial(_mlp_kernel, final=final),
        grid=(t // tm, D_FF // tf),
        in_specs=[tok, tok,
                  pl.BlockSpec((D_MODEL, tf), lambda i, f: (0, f)),
                  pl.BlockSpec((tf, D_MODEL), lambda i, f: (f, 0)),
                  pl.BlockSpec((1, D_MODEL), lambda i, f: (0, 0))],
        out_specs=tok,
        out_shape=jax.ShapeDtypeStruct((t, D_MODEL), F32),
        scratch_shapes=[pltpu.VMEM((tm, D_MODEL), F32)],
        compiler_params=_cparams(("parallel", "arbitrary")),
        name="mlp",
    )(x1, h2, w1, w2, norm_w)


def _prep_layer(l, norm_mix_w, w_in, pool_w, pool_scale, pool_proj, gdn_conv_w, gdn_a_log, gdn_dt_bias,
                gdn_norm_w, gdn_proj, conf_dw_w, conf_dw_b, conf_ln_w, conf_ln_b, conf_proj, sgu_ln_w,
                sgu_ln_b, sgu_w, sgu_b, sgu_proj, w_out, norm_mlp_w, mlp_w1, mlp_w2):
    w = w_in[l]
    o_a, o_qkv, o_z, o_ab, o_c, o_d, o_g = 0, MIX_W, 4 * MIX_W, 5 * MIX_W, 5 * MIX_W + 16, 7 * MIX_W + 16, 9 * MIX_W + 16
    w_mix = jnp.concatenate([
        w[:, o_qkv:o_z], w[:, o_a:o_qkv], w[:, o_c:o_d], w[:, o_d:o_g], w[:, o_z:o_ab], w[:, o_ab:o_c],
        jnp.zeros((D_MODEL, D_MIXP - OFF_AB - 16), F32)], axis=1).astype(BF16)
    row = lambda a: a.reshape(1, -1).astype(F32)
    pad_row = lambda a: jnp.zeros((1, LANE), F32).at[0, 2 * GDN_HEADS:4 * GDN_HEADS].set(a.reshape(-1))
    return dict(
        norm_mix_w=row(norm_mix_w[l]),
        w_mix=w_mix,
        w_gate=w[:, o_g:].astype(BF16),
        pool_w=pool_w[l].astype(BF16),
        pool_scale=row(pool_scale[l]),
        gconv_w=gdn_conv_w[l].astype(F32),
        ab_a=pad_row(-jnp.exp(gdn_a_log[l].astype(F32))),
        ab_dtb=pad_row(gdn_dt_bias[l].astype(F32)),
        gdn_norm_w=row(gdn_norm_w[l]),
        cdw_w=conf_dw_w[l].astype(F32),
        cdw_b=row(conf_dw_b[l]),
        cln_w=row(conf_ln_w[l]),
        cln_b=row(conf_ln_b[l]),
        sln_w=row(sgu_ln_w[l]),
        sln_b=row(sgu_ln_b[l]),
        sgu_w=sgu_w[l].astype(BF16),
        sgu_bias=jnp.repeat(sgu_b[l].astype(F32).T, GROUP_DIM, axis=1),
        projs=tuple(pr[l].astype(BF16) for pr in (pool_proj, gdn_proj, conf_proj, sgu_proj)),
        w_out=w_out[l].astype(BF16),
        norm_mlp_w=row(norm_mlp_w[l]),
        w1=mlp_w1[l].astype(BF16),
        w2=mlp_w2[l].astype(BF16),
    )


def _encoder(x, layers, norm_final_w):
    b, s, _ = x.shape
    x2d = x.reshape(b * s, D_MODEL)
    nf = norm_final_w.reshape(1, -1).astype(F32)
    for li, lw in enumerate(layers):
        p, h = _in_proj(x2d, lw["norm_mix_w"], lw["w_mix"])
        m_a, m_c, m_d, q, k, v, abg = _mix(p, lw, s)
        o_f = _gdn(q, k, v, abg, s, reverse=False)
        m_b = _gdn(q, k, v, abg, s, reverse=True, o_fwd=o_f, p=p, norm_w=lw["gdn_norm_w"])
        merged = _merge(h, (m_a, m_b, m_c, m_d), lw["w_gate"], lw["projs"])
        x1, h2 = _out_proj(x2d, merged, lw["w_out"], lw["norm_mlp_w"])
        x2d = _mlp(x1, h2, lw["w1"], lw["w2"], nf, final=(li == len(layers) - 1))
    return x2d.reshape(b, s, D_MODEL)


def kernel(x_prompt, x_sample, norm_mix_w, w_in, pool_w, pool_scale, pool_proj, gdn_conv_w, gdn_a_log,
           gdn_dt_bias, gdn_norm_w, gdn_proj, conf_dw_w, conf_dw_b, conf_ln_w, conf_ln_b, conf_proj,
           sgu_ln_w, sgu_ln_b, sgu_w, sgu_b, sgu_proj, w_out, norm_mlp_w, mlp_w1, mlp_w2, norm_final_w):
    depth = w_in.shape[0]
    layers = [_prep_layer(l, norm_mix_w, w_in, pool_w, pool_scale, pool_proj, gdn_conv_w, gdn_a_log,
                          gdn_dt_bias, gdn_norm_w, gdn_proj, conf_dw_w, conf_dw_b, conf_ln_w, conf_ln_b,
                          conf_proj, sgu_ln_w, sgu_ln_b, sgu_w, sgu_b, sgu_proj, w_out, norm_mlp_w,
                          mlp_w1, mlp_w2) for l in range(depth)]
    y_prompt = _encoder(x_prompt, layers, norm_final_w)
    y_sample = _encoder(x_sample, layers, norm_final_w)
    return (y_prompt, y_sample)
```

```python
import functools

import jax
import jax.numpy as jnp
from jax import lax
from jax.experimental import pallas as pl
from jax.experimental.pallas import tpu as pltpu

F32 = jnp.float32
BF16 = jnp.bfloat16

D_MODEL = 2048
N_GROUPS = 4
GROUP_DIM = 128
MIX_W = N_GROUPS * GROUP_DIM
POOL_WINDOWS = (2, 4, 8, 16)
GDN_HEADS = 4
GDN_DK = 128
GDN_CONV = 4
GDN_CHUNK = 64
CONF_WIDTH = 31
SGU_CHUNK = 128
D_FF = 4 * D_MODEL
N_BRANCH = 4
EPS = 1e-6

LANE = 128
SUBLANE = 8
HALO = 16
GDN_BLOCK = 2 * GDN_CHUNK

OFF_QKV = 0
OFF_XA = 3 * MIX_W
OFF_XC = 4 * MIX_W
OFF_XD = 6 * MIX_W
OFF_XZ = 8 * MIX_W
OFF_AB = 9 * MIX_W
D_MIXP = OFF_AB + LANE
MIX_SECTIONS = ((OFF_QKV, 3 * MIX_W), (OFF_XA, MIX_W), (OFF_XC, 2 * MIX_W), (OFF_XD, 2 * MIX_W),
                (OFF_XZ, MIX_W), (OFF_AB, LANE))

VMEM_LIMIT = 60 * 1024 * 1024


def _cparams(sem):
    return pltpu.CompilerParams(dimension_semantics=sem, vmem_limit_bytes=VMEM_LIMIT)


def _sigmoid(x):
    return 1.0 / (1.0 + jnp.exp(-x))


def _bdot(a, b):
    return jnp.dot(a.astype(BF16), b.astype(BF16), preferred_element_type=F32)


def _chunk_masks(reverse):
    n = GDN_BLOCK
    row = lax.broadcasted_iota(jnp.int32, (n, n), 0)
    col = lax.broadcasted_iota(jnp.int32, (n, n), 1)
    same = (row // GDN_CHUNK) == (col // GDN_CHUNK)
    if reverse:
        return same & (col >= row), same & (col > row)
    return same & (col <= row), same & (col < row)


def _inproj_kernel(x_ref, nw_ref, w_ref, p_ref, h_ref, *, row_chunk):
    for r in range(x_ref.shape[0] // row_chunk):
        rows = pl.ds(r * row_chunk, row_chunk)
        x = x_ref[rows, :]
        ms = jnp.mean(x * x, axis=-1, keepdims=True)
        h = ((x * lax.rsqrt(ms + EPS)) * nw_ref[...]).astype(BF16)
        h_ref[rows, :] = h
        for off, width in MIX_SECTIONS:
            p_ref[rows, off:off + width] = jnp.dot(h, w_ref[:, off:off + width], preferred_element_type=F32)


def _in_proj(x2d, norm_w, w_mix, tm=512, row_chunk=256):
    t = x2d.shape[0]
    return pl.pallas_call(
        functools.partial(_inproj_kernel, row_chunk=row_chunk),
        grid=(t // tm,),
        in_specs=[
            pl.BlockSpec((tm, D_MODEL), lambda i: (i, 0)),
            pl.BlockSpec((1, D_MODEL), lambda i: (0, 0)),
            pl.BlockSpec((D_MODEL, D_MIXP), lambda i: (0, 0), pipeline_mode=pl.Buffered(1)),
        ],
        out_specs=[
            pl.BlockSpec((tm, D_MIXP), lambda i: (i, 0)),
            pl.BlockSpec((tm, D_MODEL), lambda i: (i, 0)),
        ],
        out_shape=[
            jax.ShapeDtypeStruct((t, D_MIXP), F32),
            jax.ShapeDtypeStruct((t, D_MODEL), BF16),
        ],
        compiler_params=_cparams(("parallel",)),
        name="in_proj",
    )(x2d, norm_w, w_mix)


def _layernorm(x, w, b):
    mu = jnp.mean(x, axis=-1, keepdims=True)
    xc = x - mu
    var = jnp.mean(xc * xc, axis=-1, keepdims=True)
    return xc * lax.rsqrt(var + EPS) * w + b


def _mix_kernel(qkv_m, qkv_p, qkv_n, xa_m, xa_p, xa_n, xc_m, xc_p, xc_n, xd_m, ab_m,
                pool_w, pool_scale, gconv_w, ab_a, ab_dtb, cdw_w, cdw_b, cln_w, cln_b,
                sln_w, sln_b, sgu_w, sgu_bias,
                ma_ref, mc_ref, md_ref, q_ref, k_ref, v_ref, abg_ref, abgt_ref,
                extq, exta, extc, rotc, *, ts, seq_len):
    tiles_per_seq = seq_len // ts
    s = pl.program_id(0) % tiles_per_seq
    left_ok = s > 0
    right_ok = s < tiles_per_seq - 1

    def fill(ext, main, prev, nxt, fn):
        ext[0:HALO, :] = jnp.where(left_ok, fn(prev[...]), 0.0)
        ext[HALO:HALO + ts, :] = fn(main[...])
        ext[HALO + ts:HALO + ts + HALO, :] = jnp.where(right_ok, fn(nxt[...]), 0.0)

    def glu(xc):
        return xc[:, :MIX_W] * _sigmoid(xc[:, MIX_W:])

    fill(extq, qkv_m, qkv_p, qkv_n, lambda x: x)
    fill(exta, xa_m, xa_p, xa_n, lambda x: x)
    fill(extc, xc_m, xc_p, xc_n, glu)

    pos = s * ts + lax.broadcasted_iota(jnp.int32, (ts, 1), 0)
    for g, win in enumerate(POOL_WINDOWS):
        half = win // 2
        cols = slice(g * GROUP_DIM, (g + 1) * GROUP_DIM)
        acc = exta[pl.ds(HALO - half, ts), cols]
        for d in range(-half + 1, half):
            acc = acc + exta[pl.ds(HALO + d, ts), cols]
        lo = jnp.maximum(pos - half, 0)
        hi = jnp.minimum(pos - half + win, seq_len)
        cnt = (hi - lo).astype(F32)
        xg = exta[pl.ds(HALO, ts), cols]
        y = _bdot(acc / cnt - xg, pool_w[g])
        ma_ref[:, cols] = (y * pool_scale[:, cols]).astype(BF16)

    rot_rows = rotc.shape[1]
    for r in range(1, SUBLANE):
        rotc[r - 1] = extc[pl.ds(r, rot_rows), :]
    pad_l = (CONF_WIDTH - 1) // 2
    acc = None
    for kk in range(CONF_WIDTH):
        q8, r = divmod(HALO - pad_l + kk, SUBLANE)
        src = extc[pl.ds(q8 * SUBLANE, ts), :] if r == 0 else rotc[r - 1, pl.ds(q8 * SUBLANE, ts), :]
        term = src * cdw_w[kk:kk + 1, :]
        acc = term if acc is None else acc + term
    hc = _layernorm(acc + cdw_b[...], cln_w[...], cln_b[...])
    mc_ref[...] = (hc * _sigmoid(hc)).astype(BF16)

    xd = xd_m[...]
    gd = jax.nn.gelu(xd, approximate=True)
    u = gd[:, :MIX_W]
    vn = _layernorm(gd[:, MIX_W:], sln_w[...], sln_b[...]).astype(BF16)
    for c in range(ts // SGU_CHUNK):
        rows = slice(c * SGU_CHUNK, (c + 1) * SGU_CHUNK)
        for g in range(N_GROUPS):
            cols = slice(g * GROUP_DIM, (g + 1) * GROUP_DIM)
            sv = jnp.dot(sgu_w[g], vn[rows, cols], preferred_element_type=F32) + sgu_bias[:, cols]
            md_ref[rows, cols] = (u[rows, cols] * sv).astype(BF16)

    gpad_l = (GDN_CONV - 1) // 2
    acc = extq[pl.ds(HALO - gpad_l, ts), :] * gconv_w[0:1, :]
    for kk in range(1, GDN_CONV):
        acc = acc + extq[pl.ds(HALO - gpad_l + kk, ts), :] * gconv_w[kk:kk + 1, :]
    qkv = acc * _sigmoid(acc)
    for hd in range(GDN_HEADS):
        cols = slice(hd * GDN_DK, (hd + 1) * GDN_DK)
        qh = qkv[:, hd * GDN_DK:(hd + 1) * GDN_DK]
        kh = qkv[:, MIX_W + hd * GDN_DK:MIX_W + (hd + 1) * GDN_DK]
        q_ref[:, cols] = qh * lax.rsqrt(jnp.sum(qh * qh, axis=-1, keepdims=True) + EPS) * (GDN_DK ** -0.5)
        k_ref[:, cols] = kh * lax.rsqrt(jnp.sum(kh * kh, axis=-1, keepdims=True) + EPS)
    v_ref[...] = qkv[:, 2 * MIX_W:]

    ab = ab_m[...]
    col = lax.broadcasted_iota(jnp.int32, (GDN_BLOCK, LANE), 1)
    xs = ab + ab_dtb[...]
    softplus = jnp.maximum(xs, 0.0) + jnp.log(1.0 + jnp.exp(-jnp.abs(xs)))
    gate = ab_a[...] * softplus
    beta = _sigmoid(ab)
    tri_f = _chunk_masks(False)[0].astype(F32)
    tri_b = _chunk_masks(True)[0].astype(F32)
    hi_dot = functools.partial(jnp.dot, precision=lax.Precision.HIGHEST, preferred_element_type=F32)
    for blk in range(ts // GDN_BLOCK):
        rows = slice(blk * GDN_BLOCK, (blk + 1) * GDN_BLOCK)
        gc_f = hi_dot(tri_f, gate[rows])
        gc_b = hi_dot(tri_b, gate[rows])
        res = jnp.where(col < 2 * GDN_HEADS, beta[rows], jnp.where(col < 3 * GDN_HEADS, gc_f, gc_b))
        abg_ref[rows, :] = res
        abgt_ref[:, rows] = res.T


def _mix(p, lw, seq_len, ts=256):
    t = p.shape[0]
    hb = ts // HALO
    n_hb = t // HALO

    def main(width, off):
        return pl.BlockSpec((ts, width), lambda i: (i, off // width))

    def prev(width, off):
        return pl.BlockSpec((HALO, width), lambda i: (jnp.maximum(i * hb - 1, 0), off // width))

    def nxt(width, off):
        return pl.BlockSpec((HALO, width), lambda i: (jnp.minimum((i + 1) * hb, n_hb - 1), off // width))

    def full(a):
        nd = a.ndim
        return pl.BlockSpec(a.shape, lambda i: (0,) * nd)

    params = (lw["pool_w"], lw["pool_scale"], lw["gconv_w"], lw["ab_a"], lw["ab_dtb"], lw["cdw_w"],
              lw["cdw_b"], lw["cln_w"], lw["cln_b"], lw["sln_w"], lw["sln_b"], lw["sgu_w"], lw["sgu_bias"])
    in_specs = [
        main(3 * MIX_W, OFF_QKV), prev(3 * MIX_W, OFF_QKV), nxt(3 * MIX_W, OFF_QKV),
        main(MIX_W, OFF_XA), prev(MIX_W, OFF_XA), nxt(MIX_W, OFF_XA),
        main(2 * MIX_W, OFF_XC), prev(2 * MIX_W, OFF_XC), nxt(2 * MIX_W, OFF_XC),
        main(2 * MIX_W, OFF_XD),
        main(LANE, OFF_AB),
    ] + [full(a) for a in params]
    tok = lambda w: pl.BlockSpec((ts, w), lambda i: (i, 0))
    out_specs = [tok(MIX_W)] * 6 + [tok(LANE), pl.BlockSpec((LANE, ts), lambda i: (0, i))]
    out_shape = ([jax.ShapeDtypeStruct((t, MIX_W), BF16)] * 3 + [jax.ShapeDtypeStruct((t, MIX_W), F32)] * 3
                 + [jax.ShapeDtypeStruct((t, LANE), F32), jax.ShapeDtypeStruct((LANE, t), F32)])
    return pl.pallas_call(
        functools.partial(_mix_kernel, ts=ts, seq_len=seq_len),
        grid=(t // ts,),
        in_specs=in_specs,
        out_specs=out_specs,
        out_shape=out_shape,
        scratch_shapes=[
            pltpu.VMEM((ts + 2 * HALO, 3 * MIX_W), F32),
            pltpu.VMEM((ts + 2 * HALO, MIX_W), F32),
            pltpu.VMEM((ts + 2 * HALO, MIX_W), F32),
            pltpu.VMEM((SUBLANE - 1, ts + 2 * HALO - SUBLANE, MIX_W), F32),
        ],
        compiler_params=_cparams(("parallel",)),
        name="mix",
    )(p, p, p, p, p, p, p, p, p, p, p, *params)


def _gdn_kernel(qf, kf, vf, abf, abtf, qb, kb, vb, abb, abtb, of_ref, ob_ref, s_ref):
    @pl.when(pl.program_id(1) == 0)
    def _():
        s_ref[...] = jnp.zeros_like(s_ref)

    n = GDN_BLOCK
    rowv = lax.broadcasted_iota(jnp.int32, (n, 1), 0)
    q_refs, k_refs, v_refs = (qf, qb), (kf, kb), (vf, vb)
    ab = (abf[...], abb[...])
    abt = (abtf[...], abtb[...])
    masks = (_chunk_masks(False), _chunk_masks(True))
    last = ((GDN_CHUNK - 1, n - 1), (0, GDN_CHUNK))
    order = ((0, 1), (1, 0))
    o_refs = (of_ref, ob_ref)

    items = [(d, hd) for d in range(2) for hd in range(GDN_HEADS)]
    idx = range(len(items))
    hcols = [slice(hd * GDN_DK, (hd + 1) * GDN_DK) for _, hd in items]
    cgs = [2 * GDN_HEADS + d * GDN_HEADS + hd for d, hd in items]
    beta = [ab[d][:, d * GDN_HEADS + hd:d * GDN_HEADS + hd + 1] for d, hd in items]
    gcc = [ab[d][:, cg:cg + 1] for (d, _), cg in zip(items, cgs)]
    decay = [jnp.exp(jnp.where(masks[d][0], gcc[i] - abt[d][cgs[i]:cgs[i] + 1, :], -jnp.inf))
             for i, (d, _) in enumerate(items)]
    kh = [k_refs[d][:, hcols[i]] for i, (d, _) in enumerate(items)]
    qh = [q_refs[d][:, hcols[i]] for i, (d, _) in enumerate(items)]
    vh = [v_refs[d][:, hcols[i]] for i, (d, _) in enumerate(items)]
    kh16 = [x.astype(BF16) for x in kh]
    dn = (((1,), (1,)), ((), ()))
    gram = [lax.dot_general(kh16[i], kh16[i], dn, preferred_element_type=F32) for i in idx]
    qkm = [lax.dot_general(qh[i].astype(BF16), kh16[i], dn, preferred_element_type=F32) * decay[i] for i in idx]
    toff = [jnp.where(masks[d][1], -(beta[i] * gram[i] * decay[i]), 0.0) for i, (d, _) in enumerate(items)]
    p = toff
    for _ in range(5):
        p = [_bdot(p[i], p[i]) for i in idx]
        toff = [toff[i] + p[i] + _bdot(toff[i], p[i]) for i in idx]
    eg = [jnp.exp(g) for g in gcc]
    rhs = [jnp.concatenate([vh[i] * beta[i], kh[i] * (beta[i] * eg[i])], axis=1) for i in idx]
    uw = [rhs[i] + _bdot(toff[i], rhs[i]) for i in idx]
    qg = [qh[i] * eg[i] for i in idx]
    gl = [[ab[d][r:r + 1, cgs[i]:cgs[i] + 1] for r in last[d]] for i, (d, _) in enumerate(items)]
    kg_t = [(kh[i] * jnp.exp(jnp.where(rowv < GDN_CHUNK, gl[i][0], gl[i][1]) - gcc[i])).T for i in idx]
    state = [s_ref[d, hd] for d, hd in items]
    outs = [[None, None] for _ in idx]
    for step in range(2):
        cs = [order[d][step] for d, _ in items]
        rs = [slice(c * GDN_CHUNK, (c + 1) * GDN_CHUNK) for c in cs]
        ws = [_bdot(jnp.concatenate([uw[i][rs[i], GDN_DK:], qg[i][rs[i]]], axis=0), state[i]) for i in idx]
        v_new = [uw[i][rs[i], :GDN_DK] - ws[i][:GDN_CHUNK] for i in idx]
        for i in idx:
            outs[i][cs[i]] = ws[i][GDN_CHUNK:] + _bdot(qkm[i][rs[i], rs[i]], v_new[i])
        state = [state[i] * jnp.exp(gl[i][cs[i]]) + _bdot(kg_t[i][:, rs[i]], v_new[i]) for i in idx]
    for i, (d, hd) in enumerate(items):
        s_ref[d, hd] = state[i]
        o_refs[d][:, hcols[i]] = jnp.concatenate(outs[i], axis=0)


def _gdn(q, k, v, abg, abgt, seq_len):
    t = q.shape[0]
    n_seq = t // seq_len
    bps = seq_len // GDN_BLOCK
    fwd = lambda b, i: b * bps + i
    bwd = lambda b, i: b * bps + bps - 1 - i

    def specs(blk):
        tok = pl.BlockSpec((GDN_BLOCK, MIX_W), lambda b, i: (blk(b, i), 0))
        return [tok, tok, tok,
                pl.BlockSpec((GDN_BLOCK, LANE), lambda b, i: (blk(b, i), 0)),
                pl.BlockSpec((LANE, GDN_BLOCK), lambda b, i: (0, blk(b, i)))]

    out_sd = jax.ShapeDtypeStruct((t, MIX_W), F32)
    return pl.pallas_call(
        _gdn_kernel,
        grid=(n_seq, bps),
        in_specs=specs(fwd) + specs(bwd),
        out_specs=[specs(fwd)[0], specs(bwd)[0]],
        out_shape=[out_sd, out_sd],
        scratch_shapes=[pltpu.VMEM((2, GDN_HEADS, GDN_DK, GDN_DK), F32)],
        compiler_params=_cparams(("arbitrary", "arbitrary")),
        name="gdn",
    )(q, k, v, abg, abgt, q, k, v, abg, abgt)


def _gdnpost_kernel(of_ref, ob_ref, z_ref, nw_ref, o_ref):
    for hd in range(GDN_HEADS):
        cols = slice(hd * GDN_DK, (hd + 1) * GDN_DK)
        o = of_ref[:, cols] + ob_ref[:, cols]
        o = o * lax.rsqrt(jnp.mean(o * o, axis=-1, keepdims=True) + EPS) * nw_ref[...]
        z = z_ref[:, cols]
        o_ref[:, cols] = (o * (z * _sigmoid(z))).astype(BF16)


def _gdn_post(o_f, o_b, p, norm_w, tm=1024):
    t = o_f.shape[0]
    tok = pl.BlockSpec((tm, MIX_W), lambda i: (i, 0))
    return pl.pallas_call(
        _gdnpost_kernel,
        grid=(t // tm,),
        in_specs=[tok, tok, pl.BlockSpec((tm, MIX_W), lambda i: (i, OFF_XZ // MIX_W)),
                  pl.BlockSpec((1, GDN_DK), lambda i: (0, 0))],
        out_specs=tok,
        out_shape=jax.ShapeDtypeStruct((t, MIX_W), BF16),
        compiler_params=_cparams(("parallel",)),
        name="gdn_post",
    )(o_f, o_b, p, norm_w)


def _merge_kernel(h_ref, ma, mb, mc, md, wg0, wg1, wg2, wg3, pa, pb, pc, pd, o_ref):
    h = h_ref[...]
    acc = None
    for m_ref, wg, pr in ((ma, wg0, pa), (mb, wg1, pb), (mc, wg2, pc), (md, wg3, pd)):
        gate = _sigmoid(jnp.dot(h, wg[...], preferred_element_type=F32))
        y = jnp.dot(m_ref[...], pr[...], preferred_element_type=F32)
        acc = gate * y if acc is None else acc + gate * y
    o_ref[...] = acc.astype(BF16)


def _merge(h, ms, w_gate, projs, tm=1024, tn=512):
    t = h.shape[0]
    nj = D_MODEL // tn
    tokm = pl.BlockSpec((tm, MIX_W), lambda i, j: (i, 0))
    in_specs = [pl.BlockSpec((tm, D_MODEL), lambda i, j: (i, 0))] + [tokm] * 4
    in_specs += [pl.BlockSpec((D_MODEL, tn), functools.partial(lambda i, j, b: (0, b * nj + j), b=b))
                 for b in range(N_BRANCH)]
    in_specs += [pl.BlockSpec((MIX_W, tn), lambda i, j: (0, j))] * 4
    return pl.pallas_call(
        _merge_kernel,
        grid=(t // tm, nj),
        in_specs=in_specs,
        out_specs=pl.BlockSpec((tm, tn), lambda i, j: (i, j)),
        out_shape=jax.ShapeDtypeStruct((t, D_MODEL), BF16),
        compiler_params=_cparams(("parallel", "arbitrary")),
        name="merge",
    )(h, *ms, w_gate, w_gate, w_gate, w_gate, *projs)


def _outproj_kernel(x_ref, m_ref, w_ref, nw_ref, x1_ref, h2_ref):
    x1 = x_ref[...] + jnp.dot(m_ref[...], w_ref[...], preferred_element_type=F32)
    x1_ref[...] = x1
    ms = jnp.mean(x1 * x1, axis=-1, keepdims=True)
    h2_ref[...] = ((x1 * lax.rsqrt(ms + EPS)) * nw_ref[...]).astype(BF16)


def _out_proj(x2d, merged, w_out, norm_w, tm=512):
    t = x2d.shape[0]
    tok = pl.BlockSpec((tm, D_MODEL), lambda i: (i, 0))
    return pl.pallas_call(
        _outproj_kernel,
        grid=(t // tm,),
        in_specs=[tok, tok, pl.BlockSpec((D_MODEL, D_MODEL), lambda i: (0, 0)),
                  pl.BlockSpec((1, D_MODEL), lambda i: (0, 0))],
        out_specs=[tok, tok],
        out_shape=[jax.ShapeDtypeStruct((t, D_MODEL), F32), jax.ShapeDtypeStruct((t, D_MODEL), BF16)],
        compiler_params=_cparams(("parallel",)),
        name="out_proj",
    )(x2d, merged, w_out, norm_w)


def _mlp_kernel(x1_ref, h2_ref, w1_ref, w2_ref, nw_ref, o_ref, acc_ref, *, final):
    f = pl.program_id(1)

    @pl.when(f == 0)
    def _():
        acc_ref[...] = x1_ref[...]

    a = jnp.maximum(jnp.dot(h2_ref[...], w1_ref[...], preferred_element_type=F32), 0.0)
    acc_ref[...] += jnp.dot((a * a).astype(BF16), w2_ref[...], preferred_element_type=F32)

    @pl.when(f == pl.num_programs(1) - 1)
    def _():
        x2 = acc_ref[...]
        if final:
            ms = jnp.mean(x2 * x2, axis=-1, keepdims=True)
            x2 = (x2 * lax.rsqrt(ms + EPS)) * nw_ref[...]
        o_ref[...] = x2


def _mlp(x1, h2, w1, w2, norm_w, final, tm=512, tf=1024):
    t = x1.shape[0]
    tok = pl.BlockSpec((tm, D_MODEL), lambda i, f: (i, 0))
    return pl.pallas_call(
        functools.partial(_mlp_kernel, final=final),
        grid=(t // tm, D_FF // tf),
        in_specs=[tok, tok,
                  pl.BlockSpec((D_MODEL, tf), lambda i, f: (0, f)),
                  pl.BlockSpec((tf, D_MODEL), lambda i, f: (f, 0)),
                  pl.BlockSpec((1, D_MODEL), lambda i, f: (0, 0))],
        out_specs=tok,
        out_shape=jax.ShapeDtypeStruct((t, D_MODEL), F32),
        scratch_shapes=[pltpu.VMEM((tm, D_MODEL), F32)],
        compiler_params=_cparams(("parallel", "arbitrary")),
        name="mlp",
    )(x1, h2, w1, w2, norm_w)


def _prep_layer(l, norm_mix_w, w_in, pool_w, pool_scale, pool_proj, gdn_conv_w, gdn_a_log, gdn_dt_bias,
                gdn_norm_w, gdn_proj, conf_dw_w, conf_dw_b, conf_ln_w, conf_ln_b, conf_proj, sgu_ln_w,
                sgu_ln_b, sgu_w, sgu_b, sgu_proj, w_out, norm_mlp_w, mlp_w1, mlp_w2):
    w = w_in[l]
    o_a, o_qkv, o_z, o_ab, o_c, o_d, o_g = 0, MIX_W, 4 * MIX_W, 5 * MIX_W, 5 * MIX_W + 16, 7 * MIX_W + 16, 9 * MIX_W + 16
    w_mix = jnp.concatenate([
        w[:, o_qkv:o_z], w[:, o_a:o_qkv], w[:, o_c:o_d], w[:, o_d:o_g], w[:, o_z:o_ab], w[:, o_ab:o_c],
        jnp.zeros((D_MODEL, D_MIXP - OFF_AB - 16), F32)], axis=1).astype(BF16)
    row = lambda a: a.reshape(1, -1).astype(F32)
    pad_row = lambda a: jnp.zeros((1, LANE), F32).at[0, 2 * GDN_HEADS:4 * GDN_HEADS].set(a.reshape(-1))
    return dict(
        norm_mix_w=row(norm_mix_w[l]),
        w_mix=w_mix,
        w_gate=w[:, o_g:].astype(BF16),
        pool_w=pool_w[l].astype(BF16),
        pool_scale=row(pool_scale[l]),
        gconv_w=gdn_conv_w[l].astype(F32),
        ab_a=pad_row(-jnp.exp(gdn_a_log[l].astype(F32))),
        ab_dtb=pad_row(gdn_dt_bias[l].astype(F32)),
        gdn_norm_w=row(gdn_norm_w[l]),
        cdw_w=conf_dw_w[l].astype(F32),
        cdw_b=row(conf_dw_b[l]),
        cln_w=row(conf_ln_w[l]),
        cln_b=row(conf_ln_b[l]),
        sln_w=row(sgu_ln_w[l]),
        sln_b=row(sgu_ln_b[l]),
        sgu_w=sgu_w[l].astype(BF16),
        sgu_bias=jnp.repeat(sgu_b[l].astype(F32).T, GROUP_DIM, axis=1),
        projs=tuple(pr[l].astype(BF16) for pr in (pool_proj, gdn_proj, conf_proj, sgu_proj)),
        w_out=w_out[l].astype(BF16),
        norm_mlp_w=row(norm_mlp_w[l]),
        w1=mlp_w1[l].astype(BF16),
        w2=mlp_w2[l].astype(BF16),
    )


def _encoder(x, layers, norm_final_w):
    b, s, _ = x.shape
    x2d = x.reshape(b * s, D_MODEL)
    nf = norm_final_w.reshape(1, -1).astype(F32)
    for li, lw in enumerate(layers):
        p, h = _in_proj(x2d, lw["norm_mix_w"], lw["w_mix"])
        m_a, m_c, m_d, q, k, v, abg, abgt = _mix(p, lw, s)
        o_f, o_b = _gdn(q, k, v, abg, abgt, s)
        m_b = _gdn_post(o_f, o_b, p, lw["gdn_norm_w"])
        merged = _merge(h, (m_a, m_b, m_c, m_d), lw["w_gate"], lw["projs"])
        x1, h2 = _out_proj(x2d, merged, lw["w_out"], lw["norm_mlp_w"])
        x2d = _mlp(x1, h2, lw["w1"], lw["w2"], nf, final=(li == len(layers) - 1))
    return x2d.reshape(b, s, D_MODEL)


def kernel(x_prompt, x_sample, norm_mix_w, w_in, pool_w, pool_scale, pool_proj, gdn_conv_w, gdn_a_log,
           gdn_dt_bias, gdn_norm_w, gdn_proj, conf_dw_w, conf_dw_b, conf_ln_w, conf_ln_b, conf_proj,
           sgu_ln_w, sgu_ln_b, sgu_w, sgu_b, sgu_proj, w_out, norm_mlp_w, mlp_w1, mlp_w2, norm_final_w):
    depth = w_in.shape[0]
    layers = [_prep_layer(l, norm_mix_w, w_in, pool_w, pool_scale, pool_proj, gdn_conv_w, gdn_a_log,
                          gdn_dt_bias, gdn_norm_w, gdn_proj, conf_dw_w, conf_dw_b, conf_ln_w, conf_ln_b,
                          conf_proj, sgu_ln_w, sgu_ln_b, sgu_w, sgu_b, sgu_proj, w_out, norm_mlp_w,
                          mlp_w1, mlp_w2) for l in range(depth)]
    y_prompt = _encoder(x_prompt, layers, norm_final_w)
    y_sample = _encoder(x_sample, layers, norm_final_w)
    return (y_prompt, y_sample)
```

```python
import functools

import jax
import jax.numpy as jnp
from jax import lax
from jax.experimental import pallas as pl
from jax.experimental.pallas import tpu as pltpu

F32 = jnp.float32
BF16 = jnp.bfloat16

D_MODEL = 2048
N_GROUPS = 4
GROUP_DIM = 128
MIX_W = N_GROUPS * GROUP_DIM
POOL_WINDOWS = (2, 4, 8, 16)
GDN_HEADS = 4
GDN_DK = 128
GDN_CONV = 4
GDN_CHUNK = 64
CONF_WIDTH = 31
SGU_CHUNK = 128
D_FF = 4 * D_MODEL
N_BRANCH = 4
EPS = 1e-6

LANE = 128
SUBLANE = 8
HALO = 16
GDN_BLOCK = 2 * GDN_CHUNK
GDN_STEP_BLOCKS = 2

OFF_QKV = 0
OFF_XA = 3 * MIX_W
OFF_XC = 4 * MIX_W
OFF_XD = 6 * MIX_W
OFF_XZ = 8 * MIX_W
OFF_AB = 9 * MIX_W
D_MIXP = OFF_AB + LANE
MIX_SECTIONS = ((OFF_QKV, 3 * MIX_W), (OFF_XA, MIX_W), (OFF_XC, 2 * MIX_W), (OFF_XD, 2 * MIX_W),
                (OFF_XZ, MIX_W), (OFF_AB, LANE))

VMEM_LIMIT = 60 * 1024 * 1024


def _cparams(sem):
    return pltpu.CompilerParams(dimension_semantics=sem, vmem_limit_bytes=VMEM_LIMIT)


def _sigmoid(x):
    return 1.0 / (1.0 + jnp.exp(-x))


def _bdot(a, b):
    return jnp.dot(a.astype(BF16), b.astype(BF16), preferred_element_type=F32)


def _chunk_masks(reverse):
    n = GDN_BLOCK
    row = lax.broadcasted_iota(jnp.int32, (n, n), 0)
    col = lax.broadcasted_iota(jnp.int32, (n, n), 1)
    same = (row // GDN_CHUNK) == (col // GDN_CHUNK)
    if reverse:
        return same & (col >= row), same & (col > row)
    return same & (col <= row), same & (col < row)


def _inproj_kernel(x_ref, nw_ref, w_ref, p_ref, h_ref, *, row_chunk):
    for r in range(x_ref.shape[0] // row_chunk):
        rows = pl.ds(r * row_chunk, row_chunk)
        x = x_ref[rows, :]
        ms = jnp.mean(x * x, axis=-1, keepdims=True)
        h = ((x * lax.rsqrt(ms + EPS)) * nw_ref[...]).astype(BF16)
        h_ref[rows, :] = h
        for off, width in MIX_SECTIONS:
            p_ref[rows, off:off + width] = jnp.dot(h, w_ref[:, off:off + width], preferred_element_type=F32)


def _in_proj(x2d, norm_w, w_mix, tm=512, row_chunk=256):
    t = x2d.shape[0]
    return pl.pallas_call(
        functools.partial(_inproj_kernel, row_chunk=row_chunk),
        grid=(t // tm,),
        in_specs=[
            pl.BlockSpec((tm, D_MODEL), lambda i: (i, 0)),
            pl.BlockSpec((1, D_MODEL), lambda i: (0, 0)),
            pl.BlockSpec((D_MODEL, D_MIXP), lambda i: (0, 0), pipeline_mode=pl.Buffered(1)),
        ],
        out_specs=[
            pl.BlockSpec((tm, D_MIXP), lambda i: (i, 0)),
            pl.BlockSpec((tm, D_MODEL), lambda i: (i, 0)),
        ],
        out_shape=[
            jax.ShapeDtypeStruct((t, D_MIXP), F32),
            jax.ShapeDtypeStruct((t, D_MODEL), BF16),
        ],
        compiler_params=_cparams(("parallel",)),
        name="in_proj",
    )(x2d, norm_w, w_mix)


def _layernorm(x, w, b):
    mu = jnp.mean(x, axis=-1, keepdims=True)
    xc = x - mu
    var = jnp.mean(xc * xc, axis=-1, keepdims=True)
    return xc * lax.rsqrt(var + EPS) * w + b


def _mix_kernel(qkv_m, qkv_p, qkv_n, xa_m, xa_p, xa_n, xc_m, xc_p, xc_n, xd_m, ab_m,
                pool_w, pool_scale, gconv_w, ab_a, ab_dtb, cdw_w, cdw_b, cln_w, cln_b,
                sln_w, sln_b, sgu_w, sgu_bias,
                ma_ref, mc_ref, md_ref, qkv_ref, abg_ref, abgt_ref,
                extq, exta, extc, rotc, *, ts, seq_len):
    tiles_per_seq = seq_len // ts
    s = pl.program_id(0) % tiles_per_seq
    left_ok = s > 0
    right_ok = s < tiles_per_seq - 1

    def fill(ext, main, prev, nxt, fn):
        ext[0:HALO, :] = jnp.where(left_ok, fn(prev[...]), 0.0)
        ext[HALO:HALO + ts, :] = fn(main[...])
        ext[HALO + ts:HALO + ts + HALO, :] = jnp.where(right_ok, fn(nxt[...]), 0.0)

    def glu(xc):
        return xc[:, :MIX_W] * _sigmoid(xc[:, MIX_W:])

    fill(extq, qkv_m, qkv_p, qkv_n, lambda x: x)
    fill(exta, xa_m, xa_p, xa_n, lambda x: x)
    fill(extc, xc_m, xc_p, xc_n, glu)

    pos = s * ts + lax.broadcasted_iota(jnp.int32, (ts, 1), 0)
    for g, win in enumerate(POOL_WINDOWS):
        half = win // 2
        cols = slice(g * GROUP_DIM, (g + 1) * GROUP_DIM)
        acc = exta[pl.ds(HALO - half, ts), cols]
        for d in range(-half + 1, half):
            acc = acc + exta[pl.ds(HALO + d, ts), cols]
        lo = jnp.maximum(pos - half, 0)
        hi = jnp.minimum(pos - half + win, seq_len)
        cnt = (hi - lo).astype(F32)
        xg = exta[pl.ds(HALO, ts), cols]
        y = _bdot(acc / cnt - xg, pool_w[g])
        ma_ref[:, cols] = (y * pool_scale[:, cols]).astype(BF16)

    rot_rows = rotc.shape[1]
    for r in range(1, SUBLANE):
        rotc[r - 1] = extc[pl.ds(r, rot_rows), :]
    pad_l = (CONF_WIDTH - 1) // 2
    acc = None
    for kk in range(CONF_WIDTH):
        q8, r = divmod(HALO - pad_l + kk, SUBLANE)
        src = extc[pl.ds(q8 * SUBLANE, ts), :] if r == 0 else rotc[r - 1, pl.ds(q8 * SUBLANE, ts), :]
        term = src * cdw_w[kk:kk + 1, :]
        acc = term if acc is None else acc + term
    hc = _layernorm(acc + cdw_b[...], cln_w[...], cln_b[...])
    mc_ref[...] = (hc * _sigmoid(hc)).astype(BF16)

    xd = xd_m[...]
    gd = jax.nn.gelu(xd, approximate=True)
    u = gd[:, :MIX_W]
    vn = _layernorm(gd[:, MIX_W:], sln_w[...], sln_b[...]).astype(BF16)
    for c in range(ts // SGU_CHUNK):
        rows = slice(c * SGU_CHUNK, (c + 1) * SGU_CHUNK)
        for g in range(N_GROUPS):
            cols = slice(g * GROUP_DIM, (g + 1) * GROUP_DIM)
            sv = jnp.dot(sgu_w[g], vn[rows, cols], preferred_element_type=F32) + sgu_bias[:, cols]
            md_ref[rows, cols] = (u[rows, cols] * sv).astype(BF16)

    gpad_l = (GDN_CONV - 1) // 2
    acc = extq[pl.ds(HALO - gpad_l, ts), :] * gconv_w[0:1, :]
    for kk in range(1, GDN_CONV):
        acc = acc + extq[pl.ds(HALO - gpad_l + kk, ts), :] * gconv_w[kk:kk + 1, :]
    qkv = acc * _sigmoid(acc)
    for hd in range(GDN_HEADS):
        qc = slice(hd * GDN_DK, (hd + 1) * GDN_DK)
        kc = slice(MIX_W + hd * GDN_DK, MIX_W + (hd + 1) * GDN_DK)
        qh = qkv[:, qc]
        kh = qkv[:, kc]
        qkv_ref[:, qc] = qh * lax.rsqrt(jnp.sum(qh * qh, axis=-1, keepdims=True) + EPS) * (GDN_DK ** -0.5)
        qkv_ref[:, kc] = kh * lax.rsqrt(jnp.sum(kh * kh, axis=-1, keepdims=True) + EPS)
    qkv_ref[:, 2 * MIX_W:] = qkv[:, 2 * MIX_W:]

    ab = ab_m[...]
    col = lax.broadcasted_iota(jnp.int32, (GDN_BLOCK, LANE), 1)
    xs = ab + ab_dtb[...]
    softplus = jnp.maximum(xs, 0.0) + jnp.log(1.0 + jnp.exp(-jnp.abs(xs)))
    gate = ab_a[...] * softplus
    beta = _sigmoid(ab)
    tri_f = _chunk_masks(False)[0].astype(F32)
    tri_b = _chunk_masks(True)[0].astype(F32)
    hi_dot = functools.partial(jnp.dot, precision=lax.Precision.HIGHEST, preferred_element_type=F32)
    for blk in range(ts // GDN_BLOCK):
        rows = slice(blk * GDN_BLOCK, (blk + 1) * GDN_BLOCK)
        gc_f = hi_dot(tri_f, gate[rows])
        gc_b = hi_dot(tri_b, gate[rows])
        res = jnp.where(col < 2 * GDN_HEADS, beta[rows], jnp.where(col < 3 * GDN_HEADS, gc_f, gc_b))
        abg_ref[rows, :] = res
        abgt_ref[:, rows] = res.T


def _mix(p, lw, seq_len, ts=512):
    t = p.shape[0]
    hb = ts // HALO
    n_hb = t // HALO

    def main(width, off):
        return pl.BlockSpec((ts, width), lambda i: (i, off // width))

    def prev(width, off):
        return pl.BlockSpec((HALO, width), lambda i: (jnp.maximum(i * hb - 1, 0), off // width))

    def nxt(width, off):
        return pl.BlockSpec((HALO, width), lambda i: (jnp.minimum((i + 1) * hb, n_hb - 1), off // width))

    def full(a):
        nd = a.ndim
        return pl.BlockSpec(a.shape, lambda i: (0,) * nd)

    params = (lw["pool_w"], lw["pool_scale"], lw["gconv_w"], lw["ab_a"], lw["ab_dtb"], lw["cdw_w"],
              lw["cdw_b"], lw["cln_w"], lw["cln_b"], lw["sln_w"], lw["sln_b"], lw["sgu_w"], lw["sgu_bias"])
    in_specs = [
        main(3 * MIX_W, OFF_QKV), prev(3 * MIX_W, OFF_QKV), nxt(3 * MIX_W, OFF_QKV),
        main(MIX_W, OFF_XA), prev(MIX_W, OFF_XA), nxt(MIX_W, OFF_XA),
        main(2 * MIX_W, OFF_XC), prev(2 * MIX_W, OFF_XC), nxt(2 * MIX_W, OFF_XC),
        main(2 * MIX_W, OFF_XD),
        main(LANE, OFF_AB),
    ] + [full(a) for a in params]
    tok = lambda w: pl.BlockSpec((ts, w), lambda i: (i, 0))
    out_specs = [tok(MIX_W)] * 3 + [tok(3 * MIX_W), tok(LANE), pl.BlockSpec((LANE, ts), lambda i: (0, i))]
    out_shape = ([jax.ShapeDtypeStruct((t, MIX_W), BF16)] * 3 + [jax.ShapeDtypeStruct((t, 3 * MIX_W), F32)]
                 + [jax.ShapeDtypeStruct((t, LANE), F32), jax.ShapeDtypeStruct((LANE, t), F32)])
    return pl.pallas_call(
        functools.partial(_mix_kernel, ts=ts, seq_len=seq_len),
        grid=(t // ts,),
        in_specs=in_specs,
        out_specs=out_specs,
        out_shape=out_shape,
        scratch_shapes=[
            pltpu.VMEM((ts + 2 * HALO, 3 * MIX_W), F32),
            pltpu.VMEM((ts + 2 * HALO, MIX_W), F32),
            pltpu.VMEM((ts + 2 * HALO, MIX_W), F32),
            pltpu.VMEM((SUBLANE - 1, ts + 2 * HALO - SUBLANE, MIX_W), F32),
        ],
        compiler_params=_cparams(("parallel",)),
        name="mix",
    )(p, p, p, p, p, p, p, p, p, p, p, *params)


def _gdn_kernel(qkvf, abf, abtf, qkvb, abb, abtb, of_ref, ob_ref, s_ref):
    @pl.when(pl.program_id(1) == 0)
    def _():
        s_ref[...] = jnp.zeros_like(s_ref)

    n = GDN_BLOCK
    rowv = lax.broadcasted_iota(jnp.int32, (n, 1), 0)
    qkv_refs, ab_refs, abt_refs, o_refs = (qkvf, qkvb), (abf, abb), (abtf, abtb), (of_ref, ob_ref)
    masks = (_chunk_masks(False), _chunk_masks(True))
    last = ((GDN_CHUNK - 1, n - 1), (0, GDN_CHUNK))
    order = ((0, 1), (1, 0))

    items = [(d, hd) for d in range(2) for hd in range(GDN_HEADS)]
    idx = range(len(items))
    hcols = [slice(hd * GDN_DK, (hd + 1) * GDN_DK) for _, hd in items]
    cgs = [2 * GDN_HEADS + d * GDN_HEADS + hd for d, hd in items]
    state = [s_ref[d, hd] for d, hd in items]
    for sub in range(GDN_STEP_BLOCKS):
        rows = (pl.ds(sub * n, n), pl.ds((GDN_STEP_BLOCKS - 1 - sub) * n, n))
        ab = [ab_refs[d][rows[d], :] for d in range(2)]
        abt = [abt_refs[d][:, rows[d]] for d in range(2)]
        beta = [ab[d][:, d * GDN_HEADS + hd:d * GDN_HEADS + hd + 1] for d, hd in items]
        gcc = [ab[d][:, cg:cg + 1] for (d, _), cg in zip(items, cgs)]
        decay = [jnp.exp(jnp.where(masks[d][0], gcc[i] - abt[d][cgs[i]:cgs[i] + 1, :], -jnp.inf))
                 for i, (d, _) in enumerate(items)]
        sec = lambda d, hd, k: qkv_refs[d][rows[d], k * MIX_W + hd * GDN_DK:k * MIX_W + (hd + 1) * GDN_DK]
        qh = [sec(d, hd, 0) for d, hd in items]
        kh = [sec(d, hd, 1) for d, hd in items]
        vh = [sec(d, hd, 2) for d, hd in items]
        kh16 = [x.astype(BF16) for x in kh]
        dn = (((1,), (1,)), ((), ()))
        gram = [lax.dot_general(kh16[i], kh16[i], dn, preferred_element_type=F32) for i in idx]
        qkm = [lax.dot_general(qh[i].astype(BF16), kh16[i], dn, preferred_element_type=F32) * decay[i]
               for i in idx]
        toff = [jnp.where(masks[d][1], -(beta[i] * gram[i] * decay[i]), 0.0) for i, (d, _) in enumerate(items)]
        p = toff
        for _ in range(5):
            p = [_bdot(p[i], p[i]) for i in idx]
            toff = [toff[i] + p[i] + _bdot(toff[i], p[i]) for i in idx]
        eg = [jnp.exp(g) for g in gcc]
        rhs = [jnp.concatenate([vh[i] * beta[i], kh[i] * (beta[i] * eg[i])], axis=1) for i in idx]
        uw = [rhs[i] + _bdot(toff[i], rhs[i]) for i in idx]
        qg = [qh[i] * eg[i] for i in idx]
        gl = [[ab[d][r:r + 1, cgs[i]:cgs[i] + 1] for r in last[d]] for i, (d, _) in enumerate(items)]
        kg_t = [(kh[i] * jnp.exp(jnp.where(rowv < GDN_CHUNK, gl[i][0], gl[i][1]) - gcc[i])).T for i in idx]
        outs = [[None, None] for _ in idx]
        for step in range(2):
            cs = [order[d][step] for d, _ in items]
            rs = [slice(c * GDN_CHUNK, (c + 1) * GDN_CHUNK) for c in cs]
            ws = [_bdot(jnp.concatenate([uw[i][rs[i], GDN_DK:], qg[i][rs[i]]], axis=0), state[i]) for i in idx]
            v_new = [uw[i][rs[i], :GDN_DK] - ws[i][:GDN_CHUNK] for i in idx]
            for i in idx:
                outs[i][cs[i]] = ws[i][GDN_CHUNK:] + _bdot(qkm[i][rs[i], rs[i]], v_new[i])
            state = [state[i] * jnp.exp(gl[i][cs[i]]) + _bdot(kg_t[i][:, rs[i]], v_new[i]) for i in idx]
        for i, (d, hd) in enumerate(items):
            o_refs[d][rows[d], hcols[i]] = jnp.concatenate(outs[i], axis=0)
    for i, (d, hd) in enumerate(items):
        s_ref[d, hd] = state[i]


def _gdn(qkv, abg, abgt, seq_len):
    t = qkv.shape[0]
    n_seq = t // seq_len
    step_rows = GDN_STEP_BLOCKS * GDN_BLOCK
    sps = seq_len // step_rows
    fwd = lambda b, i: b * sps + i
    bwd = lambda b, i: b * sps + sps - 1 - i

    def tok(blk, width):
        return pl.BlockSpec((step_rows, width), lambda b, i: (blk(b, i), 0))

    def specs(blk):
        return [tok(blk, 3 * MIX_W), tok(blk, LANE), pl.BlockSpec((LANE, step_rows), lambda b, i: (0, blk(b, i)))]

    out_sd = jax.ShapeDtypeStruct((t, MIX_W), F32)
    return pl.pallas_call(
        _gdn_kernel,
        grid=(n_seq, sps),
        in_specs=specs(fwd) + specs(bwd),
        out_specs=[tok(fwd, MIX_W), tok(bwd, MIX_W)],
        out_shape=[out_sd, out_sd],
        scratch_shapes=[pltpu.VMEM((2, GDN_HEADS, GDN_DK, GDN_DK), F32)],
        compiler_params=_cparams(("arbitrary", "arbitrary")),
        name="gdn",
    )(qkv, abg, abgt, qkv, abg, abgt)


def _merge_kernel(h_ref, ma, of_ref, ob_ref, z_ref, gnw_ref, mc, md, wg0, wg1, wg2, wg3, pa, pb, pc, pd, o_ref):
    h = h_ref[...]
    mb = []
    for hd in range(GDN_HEADS):
        cols = slice(hd * GDN_DK, (hd + 1) * GDN_DK)
        o = of_ref[:, cols] + ob_ref[:, cols]
        o = o * lax.rsqrt(jnp.mean(o * o, axis=-1, keepdims=True) + EPS) * gnw_ref[...]
        z = z_ref[:, cols]
        mb.append((o * (z * _sigmoid(z))).astype(BF16))
    ms = (ma[...], jnp.concatenate(mb, axis=1), mc[...], md[...])
    acc = None
    for m, wg, pr in zip(ms, (wg0, wg1, wg2, wg3), (pa, pb, pc, pd)):
        gate = _sigmoid(jnp.dot(h, wg[...], preferred_element_type=F32))
        y = jnp.dot(m, pr[...], preferred_element_type=F32)
        acc = gate * y if acc is None else acc + gate * y
    o_ref[...] = acc.astype(BF16)


def _merge(h, m_a, o_f, o_b, p, gdn_norm_w, m_c, m_d, w_gate, projs, tm=512, tn=512):
    t = h.shape[0]
    nj = D_MODEL // tn
    tokm = pl.BlockSpec((tm, MIX_W), lambda i, j: (i, 0))
    in_specs = [pl.BlockSpec((tm, D_MODEL), lambda i, j: (i, 0)), tokm, tokm, tokm,
                pl.BlockSpec((tm, MIX_W), lambda i, j: (i, OFF_XZ // MIX_W)),
                pl.BlockSpec((1, GDN_DK), lambda i, j: (0, 0)), tokm, tokm]
    in_specs += [pl.BlockSpec((D_MODEL, tn), functools.partial(lambda i, j, b: (0, b * nj + j), b=b))
                 for b in range(N_BRANCH)]
    in_specs += [pl.BlockSpec((MIX_W, tn), lambda i, j: (0, j))] * 4
    return pl.pallas_call(
        _merge_kernel,
        grid=(t // tm, nj),
        in_specs=in_specs,
        out_specs=pl.BlockSpec((tm, tn), lambda i, j: (i, j)),
        out_shape=jax.ShapeDtypeStruct((t, D_MODEL), BF16),
        compiler_params=_cparams(("parallel", "arbitrary")),
        name="merge",
    )(h, m_a, o_f, o_b, p, gdn_norm_w, m_c, m_d, w_gate, w_gate, w_gate, w_gate, *projs)


def _outproj_kernel(x_ref, m_ref, w_ref, nw_ref, x1_ref, h2_ref):
    x1 = x_ref[...] + jnp.dot(m_ref[...], w_ref[...], preferred_element_type=F32)
    x1_ref[...] = x1
    ms = jnp.mean(x1 * x1, axis=-1, keepdims=True)
    h2_ref[...] = ((x1 * lax.rsqrt(ms + EPS)) * nw_ref[...]).astype(BF16)


def _out_proj(x2d, merged, w_out, norm_w, tm=512):
    t = x2d.shape[0]
    tok = pl.BlockSpec((tm, D_MODEL), lambda i: (i, 0))
    return pl.pallas_call(
        _outproj_kernel,
        grid=(t // tm,),
        in_specs=[tok, tok, pl.BlockSpec((D_MODEL, D_MODEL), lambda i: (0, 0)),
                  pl.BlockSpec((1, D_MODEL), lambda i: (0, 0))],
        out_specs=[tok, tok],
        out_shape=[jax.ShapeDtypeStruct((t, D_MODEL), F32), jax.ShapeDtypeStruct((t, D_MODEL), BF16)],
        compiler_params=_cparams(("parallel",)),
        name="out_proj",
    )(x2d, merged, w_out, norm_w)


def _mlp_kernel(x1_ref, h2_ref, w1_ref, w2_ref, nw_ref, o_ref, acc_ref, *, final):
    f = pl.program_id(1)

    @pl.when(f == 0)
    def _():
        acc_ref[...] = x1_ref[...]

    a = jnp.maximum(jnp.dot(h2_ref[...], w1_ref[...], preferred_element_type=F32), 0.0)
    acc_ref[...] += jnp.dot((a * a).astype(BF16), w2_ref[...], preferred_element_type=F32)

    @pl.when(f == pl.num_programs(1) - 1)
    def _():
        x2 = acc_ref[...]
        if final:
            ms = jnp.mean(x2 * x2, axis=-1, keepdims=True)
            x2 = (x2 * lax.rsqrt(ms + EPS)) * nw_ref[...]
        o_ref[...] = x2


def _mlp(x1, h2, w1, w2, norm_w, final, tm=512, tf=1024):
    t = x1.shape[0]
    tok = pl.BlockSpec((tm, D_MODEL), lambda i, f: (i, 0))
    return pl.pallas_call(
        functools.partial(_mlp_kernel, final=final),
        grid=(t // tm, D_FF // tf),
        in_specs=[tok, tok,
                  pl.BlockSpec((D_MODEL, tf), lambda i, f: (0, f)),
                  pl.BlockSpec((tf, D_MODEL), lambda i, f: (f, 0)),
                  pl.BlockSpec((1, D_MODEL), lambda i, f: (0, 0))],
        out_specs=tok,
        out_shape=jax.ShapeDtypeStruct((t, D_MODEL), F32),
        scratch_shapes=[pltpu.VMEM((tm, D_MODEL), F32)],
        compiler_params=_cparams(("parallel", "arbitrary")),
        name="mlp",
    )(x1, h2, w1, w2, norm_w)


def _prep_layer(l, norm_mix_w, w_in, pool_w, pool_scale, pool_proj, gdn_conv_w, gdn_a_log, gdn_dt_bias,
                gdn_norm_w, gdn_proj, conf_dw_w, conf_dw_b, conf_ln_w, conf_ln_b, conf_proj, sgu_ln_w,
                sgu_ln_b, sgu_w, sgu_b, sgu_proj, w_out, norm_mlp_w, mlp_w1, mlp_w2):
    w = w_in[l]
    o_a, o_qkv, o_z, o_ab, o_c, o_d, o_g = 0, MIX_W, 4 * MIX_W, 5 * MIX_W, 5 * MIX_W + 16, 7 * MIX_W + 16, 9 * MIX_W + 16
    w_mix = jnp.concatenate([
        w[:, o_qkv:o_z], w[:, o_a:o_qkv], w[:, o_c:o_d], w[:, o_d:o_g], w[:, o_z:o_ab], w[:, o_ab:o_c],
        jnp.zeros((D_MODEL, D_MIXP - OFF_AB - 16), F32)], axis=1).astype(BF16)
    row = lambda a: a.reshape(1, -1).astype(F32)
    pad_row = lambda a: jnp.zeros((1, LANE), F32).at[0, 2 * GDN_HEADS:4 * GDN_HEADS].set(a.reshape(-1))
    return dict(
        norm_mix_w=row(norm_mix_w[l]),
        w_mix=w_mix,
        w_gate=w[:, o_g:].astype(BF16),
        pool_w=pool_w[l].astype(BF16),
        pool_scale=row(pool_scale[l]),
        gconv_w=gdn_conv_w[l].astype(F32),
        ab_a=pad_row(-jnp.exp(gdn_a_log[l].astype(F32))),
        ab_dtb=pad_row(gdn_dt_bias[l].astype(F32)),
        gdn_norm_w=row(gdn_norm_w[l]),
        cdw_w=conf_dw_w[l].astype(F32),
        cdw_b=row(conf_dw_b[l]),
        cln_w=row(conf_ln_w[l]),
        cln_b=row(conf_ln_b[l]),
        sln_w=row(sgu_ln_w[l]),
        sln_b=row(sgu_ln_b[l]),
        sgu_w=sgu_w[l].astype(BF16),
        sgu_bias=jnp.repeat(sgu_b[l].astype(F32).T, GROUP_DIM, axis=1),
        projs=tuple(pr[l].astype(BF16) for pr in (pool_proj, gdn_proj, conf_proj, sgu_proj)),
        w_out=w_out[l].astype(BF16),
        norm_mlp_w=row(norm_mlp_w[l]),
        w1=mlp_w1[l].astype(BF16),
        w2=mlp_w2[l].astype(BF16),
    )


def _encoder(x, layers, norm_final_w):
    b, s, _ = x.shape
    x2d = x.reshape(b * s, D_MODEL)
    nf = norm_final_w.reshape(1, -1).astype(F32)
    for li, lw in enumerate(layers):
        p, h = _in_proj(x2d, lw["norm_mix_w"], lw["w_mix"])
        m_a, m_c, m_d, qkv, abg, abgt = _mix(p, lw, s)
        o_f, o_b = _gdn(qkv, abg, abgt, s)
        merged = _merge(h, m_a, o_f, o_b, p, lw["gdn_norm_w"], m_c, m_d, lw["w_gate"], lw["projs"])
        x1, h2 = _out_proj(x2d, merged, lw["w_out"], lw["norm_mlp_w"])
        x2d = _mlp(x1, h2, lw["w1"], lw["w2"], nf, final=(li == len(layers) - 1))
    return x2d.reshape(b, s, D_MODEL)


def kernel(x_prompt, x_sample, norm_mix_w, w_in, pool_w, pool_scale, pool_proj, gdn_conv_w, gdn_a_log,
           gdn_dt_bias, gdn_norm_w, gdn_proj, conf_dw_w, conf_dw_b, conf_ln_w, conf_ln_b, conf_proj,
           sgu_ln_w, sgu_ln_b, sgu_w, sgu_b, sgu_proj, w_out, norm_mlp_w, mlp_w1, mlp_w2, norm_final_w):
    depth = w_in.shape[0]
    layers = [_prep_layer(l, norm_mix_w, w_in, pool_w, pool_scale, pool_proj, gdn_conv_w, gdn_a_log,
                          gdn_dt_bias, gdn_norm_w, gdn_proj, conf_dw_w, conf_dw_b, conf_ln_w, conf_ln_b,
                          conf_proj, sgu_ln_w, sgu_ln_b, sgu_w, sgu_b, sgu_proj, w_out, norm_mlp_w,
                          mlp_w1, mlp_w2) for l in range(depth)]
    y_prompt = _encoder(x_prompt, layers, norm_final_w)
    y_sample = _encoder(x_sample, layers, norm_final_w)
    return (y_prompt, y_sample)
```

```python
import functools

import jax
import jax.numpy as jnp
from jax import lax
from jax.experimental import pallas as pl
from jax.experimental.pallas import tpu as pltpu

F32 = jnp.float32
BF16 = jnp.bfloat16

D_MODEL = 2048
N_GROUPS = 4
GROUP_DIM = 128
MIX_W = N_GROUPS * GROUP_DIM
POOL_WINDOWS = (2, 4, 8, 16)
GDN_HEADS = 4
GDN_DK = 128
GDN_CONV = 4
GDN_CHUNK = 64
CONF_WIDTH = 31
SGU_CHUNK = 128
D_FF = 4 * D_MODEL
N_BRANCH = 4
EPS = 1e-6

LANE = 128
SUBLANE = 8
HALO = 16
GDN_BLOCK = 2 * GDN_CHUNK
GDN_STEP_BLOCKS = 2

OFF_QKV = 0
OFF_XA = 3 * MIX_W
OFF_XC = 4 * MIX_W
OFF_XD = 6 * MIX_W
OFF_XZ = 8 * MIX_W
OFF_AB = 9 * MIX_W
D_MIXP = OFF_AB + LANE
MIX_SECTIONS = ((OFF_QKV, 3 * MIX_W), (OFF_XA, MIX_W), (OFF_XC, 2 * MIX_W), (OFF_XD, 2 * MIX_W),
                (OFF_XZ, MIX_W), (OFF_AB, LANE))

VMEM_LIMIT = 60 * 1024 * 1024


def _cparams(sem):
    return pltpu.CompilerParams(dimension_semantics=sem, vmem_limit_bytes=VMEM_LIMIT)


def _sigmoid(x):
    return 1.0 / (1.0 + jnp.exp(-x))


def _bdot(a, b):
    return jnp.dot(a.astype(BF16), b.astype(BF16), preferred_element_type=F32)


def _chunk_masks(reverse):
    n = GDN_BLOCK
    row = lax.broadcasted_iota(jnp.int32, (n, n), 0)
    col = lax.broadcasted_iota(jnp.int32, (n, n), 1)
    same = (row // GDN_CHUNK) == (col // GDN_CHUNK)
    if reverse:
        return same & (col >= row), same & (col > row)
    return same & (col <= row), same & (col < row)


def _inproj_kernel(x_ref, nw_ref, w_ref, p_ref, h_ref, *, row_chunk):
    for r in range(x_ref.shape[0] // row_chunk):
        rows = pl.ds(r * row_chunk, row_chunk)
        x = x_ref[rows, :]
        ms = jnp.mean(x * x, axis=-1, keepdims=True)
        h = ((x * lax.rsqrt(ms + EPS)) * nw_ref[...]).astype(BF16)
        h_ref[rows, :] = h
        for off, width in MIX_SECTIONS:
            p_ref[rows, off:off + width] = jnp.dot(h, w_ref[:, off:off + width], preferred_element_type=F32)


def _in_proj(x2d, norm_w, w_mix, tm=512, row_chunk=256):
    t = x2d.shape[0]
    return pl.pallas_call(
        functools.partial(_inproj_kernel, row_chunk=row_chunk),
        grid=(t // tm,),
        in_specs=[
            pl.BlockSpec((tm, D_MODEL), lambda i: (i, 0)),
            pl.BlockSpec((1, D_MODEL), lambda i: (0, 0)),
            pl.BlockSpec((D_MODEL, D_MIXP), lambda i: (0, 0), pipeline_mode=pl.Buffered(1)),
        ],
        out_specs=[
            pl.BlockSpec((tm, D_MIXP), lambda i: (i, 0)),
            pl.BlockSpec((tm, D_MODEL), lambda i: (i, 0)),
        ],
        out_shape=[
            jax.ShapeDtypeStruct((t, D_MIXP), F32),
            jax.ShapeDtypeStruct((t, D_MODEL), BF16),
        ],
        compiler_params=_cparams(("parallel",)),
        name="in_proj",
    )(x2d, norm_w, w_mix)


def _layernorm(x, w, b):
    mu = jnp.mean(x, axis=-1, keepdims=True)
    xc = x - mu
    var = jnp.mean(xc * xc, axis=-1, keepdims=True)
    return xc * lax.rsqrt(var + EPS) * w + b


def _mix_kernel(qkv_m, qkv_p, qkv_n, xa_m, xa_p, xa_n, xc_m, xc_p, xc_n, xd_m, ab_m,
                pool_w, pool_scale, gconv_w, ab_a, ab_dtb, cdw_w, cdw_b, cln_w, cln_b,
                sln_w, sln_b, sgu_w, sgu_bias,
                ma_ref, mc_ref, md_ref, qkv_ref, abg_ref, abgt_ref,
                extq, exta, extc, rotc, *, ts, seq_len):
    tiles_per_seq = seq_len // ts
    s = pl.program_id(0) % tiles_per_seq
    left_ok = s > 0
    right_ok = s < tiles_per_seq - 1

    def fill(ext, main, prev, nxt, fn):
        ext[0:HALO, :] = jnp.where(left_ok, fn(prev[...]), 0.0)
        ext[HALO:HALO + ts, :] = fn(main[...])
        ext[HALO + ts:HALO + ts + HALO, :] = jnp.where(right_ok, fn(nxt[...]), 0.0)

    def glu(xc):
        return xc[:, :MIX_W] * _sigmoid(xc[:, MIX_W:])

    fill(extq, qkv_m, qkv_p, qkv_n, lambda x: x)
    fill(exta, xa_m, xa_p, xa_n, lambda x: x)
    fill(extc, xc_m, xc_p, xc_n, glu)

    pos = s * ts + lax.broadcasted_iota(jnp.int32, (ts, 1), 0)
    for g, win in enumerate(POOL_WINDOWS):
        half = win // 2
        cols = slice(g * GROUP_DIM, (g + 1) * GROUP_DIM)
        acc = exta[pl.ds(HALO - half, ts), cols]
        for d in range(-half + 1, half):
            acc = acc + exta[pl.ds(HALO + d, ts), cols]
        lo = jnp.maximum(pos - half, 0)
        hi = jnp.minimum(pos - half + win, seq_len)
        cnt = (hi - lo).astype(F32)
        xg = exta[pl.ds(HALO, ts), cols]
        y = _bdot(acc / cnt - xg, pool_w[g])
        ma_ref[:, cols] = (y * pool_scale[:, cols]).astype(BF16)

    rot_rows = rotc.shape[1]
    for r in range(1, SUBLANE):
        rotc[r - 1] = extc[pl.ds(r, rot_rows), :]
    pad_l = (CONF_WIDTH - 1) // 2
    acc = None
    for kk in range(CONF_WIDTH):
        q8, r = divmod(HALO - pad_l + kk, SUBLANE)
        src = extc[pl.ds(q8 * SUBLANE, ts), :] if r == 0 else rotc[r - 1, pl.ds(q8 * SUBLANE, ts), :]
        term = src * cdw_w[kk:kk + 1, :]
        acc = term if acc is None else acc + term
    hc = _layernorm(acc + cdw_b[...], cln_w[...], cln_b[...])
    mc_ref[...] = (hc * _sigmoid(hc)).astype(BF16)

    xd = xd_m[...]
    gd = jax.nn.gelu(xd, approximate=True)
    u = gd[:, :MIX_W]
    vn = _layernorm(gd[:, MIX_W:], sln_w[...], sln_b[...]).astype(BF16)
    for c in range(ts // SGU_CHUNK):
        rows = slice(c * SGU_CHUNK, (c + 1) * SGU_CHUNK)
        for g in range(N_GROUPS):
            cols = slice(g * GROUP_DIM, (g + 1) * GROUP_DIM)
            sv = jnp.dot(sgu_w[g], vn[rows, cols], preferred_element_type=F32) + sgu_bias[:, cols]
            md_ref[rows, cols] = (u[rows, cols] * sv).astype(BF16)

    gpad_l = (GDN_CONV - 1) // 2
    acc = extq[pl.ds(HALO - gpad_l, ts), :] * gconv_w[0:1, :]
    for kk in range(1, GDN_CONV):
        acc = acc + extq[pl.ds(HALO - gpad_l + kk, ts), :] * gconv_w[kk:kk + 1, :]
    qkv = acc * _sigmoid(acc)
    for hd in range(GDN_HEADS):
        qc = slice(hd * GDN_DK, (hd + 1) * GDN_DK)
        kc = slice(MIX_W + hd * GDN_DK, MIX_W + (hd + 1) * GDN_DK)
        qh = qkv[:, qc]
        kh = qkv[:, kc]
        qkv_ref[:, qc] = qh * lax.rsqrt(jnp.sum(qh * qh, axis=-1, keepdims=True) + EPS) * (GDN_DK ** -0.5)
        qkv_ref[:, kc] = kh * lax.rsqrt(jnp.sum(kh * kh, axis=-1, keepdims=True) + EPS)
    qkv_ref[:, 2 * MIX_W:] = qkv[:, 2 * MIX_W:]

    ab = ab_m[...]
    col = lax.broadcasted_iota(jnp.int32, (GDN_BLOCK, LANE), 1)
    xs = ab + ab_dtb[...]
    softplus = jnp.maximum(xs, 0.0) + jnp.log(1.0 + jnp.exp(-jnp.abs(xs)))
    gate = ab_a[...] * softplus
    beta = _sigmoid(ab)
    tri_f = _chunk_masks(False)[0].astype(F32)
    tri_b = _chunk_masks(True)[0].astype(F32)
    hi_dot = functools.partial(jnp.dot, precision=lax.Precision.HIGHEST, preferred_element_type=F32)
    for blk in range(ts // GDN_BLOCK):
        rows = slice(blk * GDN_BLOCK, (blk + 1) * GDN_BLOCK)
        gc_f = hi_dot(tri_f, gate[rows])
        gc_b = hi_dot(tri_b, gate[rows])
        res = jnp.where(col < 2 * GDN_HEADS, beta[rows], jnp.where(col < 3 * GDN_HEADS, gc_f, gc_b))
        abg_ref[rows, :] = res
        abgt_ref[:, rows] = res.T


def _mix(p, lw, seq_len, ts=512):
    t = p.shape[0]
    hb = ts // HALO
    n_hb = t // HALO

    def main(width, off):
        return pl.BlockSpec((ts, width), lambda i: (i, off // width))

    def prev(width, off):
        return pl.BlockSpec((HALO, width), lambda i: (jnp.maximum(i * hb - 1, 0), off // width))

    def nxt(width, off):
        return pl.BlockSpec((HALO, width), lambda i: (jnp.minimum((i + 1) * hb, n_hb - 1), off // width))

    def full(a):
        nd = a.ndim
        return pl.BlockSpec(a.shape, lambda i: (0,) * nd)

    params = (lw["pool_w"], lw["pool_scale"], lw["gconv_w"], lw["ab_a"], lw["ab_dtb"], lw["cdw_w"],
              lw["cdw_b"], lw["cln_w"], lw["cln_b"], lw["sln_w"], lw["sln_b"], lw["sgu_w"], lw["sgu_bias"])
    in_specs = [
        main(3 * MIX_W, OFF_QKV), prev(3 * MIX_W, OFF_QKV), nxt(3 * MIX_W, OFF_QKV),
        main(MIX_W, OFF_XA), prev(MIX_W, OFF_XA), nxt(MIX_W, OFF_XA),
        main(2 * MIX_W, OFF_XC), prev(2 * MIX_W, OFF_XC), nxt(2 * MIX_W, OFF_XC),
        main(2 * MIX_W, OFF_XD),
        main(LANE, OFF_AB),
    ] + [full(a) for a in params]
    tok = lambda w: pl.BlockSpec((ts, w), lambda i: (i, 0))
    out_specs = [tok(MIX_W)] * 3 + [tok(3 * MIX_W), tok(LANE), pl.BlockSpec((LANE, ts), lambda i: (0, i))]
    out_shape = ([jax.ShapeDtypeStruct((t, MIX_W), BF16)] * 3 + [jax.ShapeDtypeStruct((t, 3 * MIX_W), F32)]
                 + [jax.ShapeDtypeStruct((t, LANE), F32), jax.ShapeDtypeStruct((LANE, t), F32)])
    return pl.pallas_call(
        functools.partial(_mix_kernel, ts=ts, seq_len=seq_len),
        grid=(t // ts,),
        in_specs=in_specs,
        out_specs=out_specs,
        out_shape=out_shape,
        scratch_shapes=[
            pltpu.VMEM((ts + 2 * HALO, 3 * MIX_W), F32),
            pltpu.VMEM((ts + 2 * HALO, MIX_W), F32),
            pltpu.VMEM((ts + 2 * HALO, MIX_W), F32),
            pltpu.VMEM((SUBLANE - 1, ts + 2 * HALO - SUBLANE, MIX_W), F32),
        ],
        compiler_params=_cparams(("parallel",)),
        name="mix",
    )(p, p, p, p, p, p, p, p, p, p, p, *params)


def _gdn_kernel(qkvf, abf, abtf, qkvb, abb, abtb, of_ref, ob_ref, s_ref):
    @pl.when(pl.program_id(1) == 0)
    def _():
        s_ref[...] = jnp.zeros_like(s_ref)

    n = GDN_BLOCK
    rowv = lax.broadcasted_iota(jnp.int32, (n, 1), 0)
    qkv_refs, ab_refs, abt_refs, o_refs = (qkvf, qkvb), (abf, abb), (abtf, abtb), (of_ref, ob_ref)
    masks = (_chunk_masks(False), _chunk_masks(True))
    last = ((GDN_CHUNK - 1, n - 1), (0, GDN_CHUNK))
    order = ((0, 1), (1, 0))

    items = [(d, hd) for d in range(2) for hd in range(GDN_HEADS)]
    idx = range(len(items))
    hcols = [slice(hd * GDN_DK, (hd + 1) * GDN_DK) for _, hd in items]
    cgs = [2 * GDN_HEADS + d * GDN_HEADS + hd for d, hd in items]
    state = [s_ref[d, hd] for d, hd in items]
    for sub in range(GDN_STEP_BLOCKS):
        rows = (pl.ds(sub * n, n), pl.ds((GDN_STEP_BLOCKS - 1 - sub) * n, n))
        ab = [ab_refs[d][rows[d], :] for d in range(2)]
        abt = [abt_refs[d][:, rows[d]] for d in range(2)]
        beta = [ab[d][:, d * GDN_HEADS + hd:d * GDN_HEADS + hd + 1] for d, hd in items]
        gcc = [ab[d][:, cg:cg + 1] for (d, _), cg in zip(items, cgs)]
        decay = [jnp.exp(jnp.where(masks[d][0], gcc[i] - abt[d][cgs[i]:cgs[i] + 1, :], -jnp.inf))
                 for i, (d, _) in enumerate(items)]
        sec = lambda d, hd, k: qkv_refs[d][rows[d], k * MIX_W + hd * GDN_DK:k * MIX_W + (hd + 1) * GDN_DK]
        qh = [sec(d, hd, 0) for d, hd in items]
        kh = [sec(d, hd, 1) for d, hd in items]
        vh = [sec(d, hd, 2) for d, hd in items]
        kh16 = [x.astype(BF16) for x in kh]
        dn = (((1,), (1,)), ((), ()))
        gram = [lax.dot_general(kh16[i], kh16[i], dn, preferred_element_type=F32) for i in idx]
        qkm = [lax.dot_general(qh[i].astype(BF16), kh16[i], dn, preferred_element_type=F32) * decay[i]
               for i in idx]
        toff = [jnp.where(masks[d][1], -(beta[i] * gram[i] * decay[i]), 0.0) for i, (d, _) in enumerate(items)]
        p16 = [x.astype(BF16) for x in toff]
        for _ in range(5):
            p = [jnp.dot(p16[i], p16[i], preferred_element_type=F32) for i in idx]
            p16 = [x.astype(BF16) for x in p]
            toff = [toff[i] + p[i] + jnp.dot(toff[i].astype(BF16), p16[i], preferred_element_type=F32)
                    for i in idx]
        eg = [jnp.exp(g) for g in gcc]
        rhs = [jnp.concatenate([vh[i] * beta[i], kh[i] * (beta[i] * eg[i])], axis=1) for i in idx]
        uw = [rhs[i] + _bdot(toff[i], rhs[i]) for i in idx]
        qg = [qh[i] * eg[i] for i in idx]
        gl = [[ab[d][r:r + 1, cgs[i]:cgs[i] + 1] for r in last[d]] for i, (d, _) in enumerate(items)]
        kg_t = [(kh[i] * jnp.exp(jnp.where(rowv < GDN_CHUNK, gl[i][0], gl[i][1]) - gcc[i])).T for i in idx]
        outs = [[None, None] for _ in idx]
        for step in range(2):
            cs = [order[d][step] for d, _ in items]
            rs = [slice(c * GDN_CHUNK, (c + 1) * GDN_CHUNK) for c in cs]
            ws = [_bdot(jnp.concatenate([uw[i][rs[i], GDN_DK:], qg[i][rs[i]]], axis=0), state[i]) for i in idx]
            v_new = [uw[i][rs[i], :GDN_DK] - ws[i][:GDN_CHUNK] for i in idx]
            for i in idx:
                outs[i][cs[i]] = ws[i][GDN_CHUNK:] + _bdot(qkm[i][rs[i], rs[i]], v_new[i])
            state = [state[i] * jnp.exp(gl[i][cs[i]]) + _bdot(kg_t[i][:, rs[i]], v_new[i]) for i in idx]
        for i, (d, hd) in enumerate(items):
            o_refs[d][rows[d], hcols[i]] = jnp.concatenate(outs[i], axis=0)
    for i, (d, hd) in enumerate(items):
        s_ref[d, hd] = state[i]


def _gdn(qkv, abg, abgt, seq_len):
    t = qkv.shape[0]
    n_seq = t // seq_len
    step_rows = GDN_STEP_BLOCKS * GDN_BLOCK
    sps = seq_len // step_rows
    fwd = lambda b, i: b * sps + i
    bwd = lambda b, i: b * sps + sps - 1 - i

    def tok(blk, width):
        return pl.BlockSpec((step_rows, width), lambda b, i: (blk(b, i), 0))

    def specs(blk):
        return [tok(blk, 3 * MIX_W), tok(blk, LANE), pl.BlockSpec((LANE, step_rows), lambda b, i: (0, blk(b, i)))]

    out_sd = jax.ShapeDtypeStruct((t, MIX_W), F32)
    return pl.pallas_call(
        _gdn_kernel,
        grid=(n_seq, sps),
        in_specs=specs(fwd) + specs(bwd),
        out_specs=[tok(fwd, MIX_W), tok(bwd, MIX_W)],
        out_shape=[out_sd, out_sd],
        scratch_shapes=[pltpu.VMEM((2, GDN_HEADS, GDN_DK, GDN_DK), F32)],
        compiler_params=_cparams(("arbitrary", "arbitrary")),
        name="gdn",
    )(qkv, abg, abgt, qkv, abg, abgt)


def _merge_kernel(h_ref, ma, of_ref, ob_ref, z_ref, gnw_ref, mc, md, wg0, wg1, wg2, wg3, pa, pb, pc, pd, o_ref,
                  mb_ref):
    @pl.when(pl.program_id(1) == 0)
    def _():
        for hd in range(GDN_HEADS):
            cols = slice(hd * GDN_DK, (hd + 1) * GDN_DK)
            o = of_ref[:, cols] + ob_ref[:, cols]
            o = o * lax.rsqrt(jnp.mean(o * o, axis=-1, keepdims=True) + EPS) * gnw_ref[...]
            z = z_ref[:, cols]
            mb_ref[:, cols] = (o * (z * _sigmoid(z))).astype(BF16)

    h = h_ref[...]
    acc = None
    for m_ref, wg, pr in ((ma, wg0, pa), (mb_ref, wg1, pb), (mc, wg2, pc), (md, wg3, pd)):
        gate = _sigmoid(jnp.dot(h, wg[...], preferred_element_type=F32))
        y = jnp.dot(m_ref[...], pr[...], preferred_element_type=F32)
        acc = gate * y if acc is None else acc + gate * y
    o_ref[...] = acc.astype(BF16)


def _merge(h, m_a, o_f, o_b, p, gdn_norm_w, m_c, m_d, w_gate, projs, tm=512, tn=512):
    t = h.shape[0]
    nj = D_MODEL // tn
    tokm = pl.BlockSpec((tm, MIX_W), lambda i, j: (i, 0))
    in_specs = [pl.BlockSpec((tm, D_MODEL), lambda i, j: (i, 0)), tokm, tokm, tokm,
                pl.BlockSpec((tm, MIX_W), lambda i, j: (i, OFF_XZ // MIX_W)),
                pl.BlockSpec((1, GDN_DK), lambda i, j: (0, 0)), tokm, tokm]
    in_specs += [pl.BlockSpec((D_MODEL, tn), functools.partial(lambda i, j, b: (0, b * nj + j), b=b))
                 for b in range(N_BRANCH)]
    in_specs += [pl.BlockSpec((MIX_W, tn), lambda i, j: (0, j))] * 4
    return pl.pallas_call(
        _merge_kernel,
        grid=(t // tm, nj),
        in_specs=in_specs,
        out_specs=pl.BlockSpec((tm, tn), lambda i, j: (i, j)),
        out_shape=jax.ShapeDtypeStruct((t, D_MODEL), BF16),
        scratch_shapes=[pltpu.VMEM((tm, MIX_W), BF16)],
        compiler_params=_cparams(("parallel", "arbitrary")),
        name="merge",
    )(h, m_a, o_f, o_b, p, gdn_norm_w, m_c, m_d, w_gate, w_gate, w_gate, w_gate, *projs)


def _outproj_kernel(x_ref, m_ref, w_ref, nw_ref, x1_ref, h2_ref):
    x1 = x_ref[...] + jnp.dot(m_ref[...], w_ref[...], preferred_element_type=F32)
    x1_ref[...] = x1
    ms = jnp.mean(x1 * x1, axis=-1, keepdims=True)
    h2_ref[...] = ((x1 * lax.rsqrt(ms + EPS)) * nw_ref[...]).astype(BF16)


def _out_proj(x2d, merged, w_out, norm_w, tm=512):
    t = x2d.shape[0]
    tok = pl.BlockSpec((tm, D_MODEL), lambda i: (i, 0))
    return pl.pallas_call(
        _outproj_kernel,
        grid=(t // tm,),
        in_specs=[tok, tok, pl.BlockSpec((D_MODEL, D_MODEL), lambda i: (0, 0)),
                  pl.BlockSpec((1, D_MODEL), lambda i: (0, 0))],
        out_specs=[tok, tok],
        out_shape=[jax.ShapeDtypeStruct((t, D_MODEL), F32), jax.ShapeDtypeStruct((t, D_MODEL), BF16)],
        compiler_params=_cparams(("parallel",)),
        name="out_proj",
    )(x2d, merged, w_out, norm_w)


def _mlp_kernel(x1_ref, h2_ref, w1_ref, w2_ref, nw_ref, o_ref, *, final, col_chunk):
    f = pl.program_id(1)

    @pl.when(f == 0)
    def _():
        o_ref[...] = x1_ref[...]

    a = jnp.maximum(jnp.dot(h2_ref[...], w1_ref[...], preferred_element_type=F32), 0.0)
    a = (a * a).astype(BF16)
    for c in range(0, D_MODEL, col_chunk):
        o_ref[:, c:c + col_chunk] += jnp.dot(a, w2_ref[:, c:c + col_chunk], preferred_element_type=F32)

    if final:
        @pl.when(f == pl.num_programs(1) - 1)
        def _():
            x2 = o_ref[...]
            ms = jnp.mean(x2 * x2, axis=-1, keepdims=True)
            o_ref[...] = (x2 * lax.rsqrt(ms + EPS)) * nw_ref[...]


def _mlp(x1, h2, w1, w2, norm_w, final, tm=512, tf=1024, col_chunk=512):
    t = x1.shape[0]
    tok = pl.BlockSpec((tm, D_MODEL), lambda i, f: (i, 0))
    return pl.pallas_call(
        functools.partial(_mlp_kernel, final=final, col_chunk=col_chunk),
        grid=(t // tm, D_FF // tf),
        in_specs=[tok, tok,
                  pl.BlockSpec((D_MODEL, tf), lambda i, f: (0, f)),
                  pl.BlockSpec((tf, D_MODEL), lambda i, f: (f, 0)),
                  pl.BlockSpec((1, D_MODEL), lambda i, f: (0, 0))],
        out_specs=tok,
        out_shape=jax.ShapeDtypeStruct((t, D_MODEL), F32),
        compiler_params=_cparams(("parallel", "arbitrary")),
        name="mlp",
    )(x1, h2, w1, w2, norm_w)


def _prep_layer(l, norm_mix_w, w_in, pool_w, pool_scale, pool_proj, gdn_conv_w, gdn_a_log, gdn_dt_bias,
                gdn_norm_w, gdn_proj, conf_dw_w, conf_dw_b, conf_ln_w, conf_ln_b, conf_proj, sgu_ln_w,
                sgu_ln_b, sgu_w, sgu_b, sgu_proj, w_out, norm_mlp_w, mlp_w1, mlp_w2):
    w = w_in[l].astype(BF16)
    o_a, o_qkv, o_z, o_ab, o_c, o_d, o_g = 0, MIX_W, 4 * MIX_W, 5 * MIX_W, 5 * MIX_W + 16, 7 * MIX_W + 16, 9 * MIX_W + 16
    w_mix = jnp.concatenate([
        w[:, o_qkv:o_z], w[:, o_a:o_qkv], w[:, o_c:o_d], w[:, o_d:o_g], w[:, o_z:o_ab], w[:, o_ab:o_c],
        jnp.zeros((D_MODEL, D_MIXP - OFF_AB - 16), BF16)], axis=1)
    row = lambda a: a.reshape(1, -1).astype(F32)
    pad_row = lambda a: jnp.zeros((1, LANE), F32).at[0, 2 * GDN_HEADS:4 * GDN_HEADS].set(a.reshape(-1))
    return dict(
        norm_mix_w=row(norm_mix_w[l]),
        w_mix=w_mix,
        w_gate=w[:, o_g:],
        pool_w=pool_w[l].astype(BF16),
        pool_scale=row(pool_scale[l]),
        gconv_w=gdn_conv_w[l].astype(F32),
        ab_a=pad_row(-jnp.exp(gdn_a_log[l].astype(F32))),
        ab_dtb=pad_row(gdn_dt_bias[l].astype(F32)),
        gdn_norm_w=row(gdn_norm_w[l]),
        cdw_w=conf_dw_w[l].astype(F32),
        cdw_b=row(conf_dw_b[l]),
        cln_w=row(conf_ln_w[l]),
        cln_b=row(conf_ln_b[l]),
        sln_w=row(sgu_ln_w[l]),
        sln_b=row(sgu_ln_b[l]),
        sgu_w=sgu_w[l].astype(BF16),
        sgu_bias=jnp.repeat(sgu_b[l].astype(F32).T, GROUP_DIM, axis=1),
        projs=tuple(pr[l].astype(BF16) for pr in (pool_proj, gdn_proj, conf_proj, sgu_proj)),
        w_out=w_out[l].astype(BF16),
        norm_mlp_w=row(norm_mlp_w[l]),
        w1=mlp_w1[l].astype(BF16),
        w2=mlp_w2[l].astype(BF16),
    )


def _encoder(x, layers, norm_final_w):
    b, s, _ = x.shape
    x2d = x.reshape(b * s, D_MODEL)
    nf = norm_final_w.reshape(1, -1).astype(F32)
    for li, lw in enumerate(layers):
        p, h = _in_proj(x2d, lw["norm_mix_w"], lw["w_mix"])
        m_a, m_c, m_d, qkv, abg, abgt = _mix(p, lw, s)
        o_f, o_b = _gdn(qkv, abg, abgt, s)
        merged = _merge(h, m_a, o_f, o_b, p, lw["gdn_norm_w"], m_c, m_d, lw["w_gate"], lw["projs"])
        x1, h2 = _out_proj(x2d, merged, lw["w_out"], lw["norm_mlp_w"])
        x2d = _mlp(x1, h2, lw["w1"], lw["w2"], nf, final=(li == len(layers) - 1))
    return x2d.reshape(b, s, D_MODEL)


def kernel(x_prompt, x_sample, norm_mix_w, w_in, pool_w, pool_scale, pool_proj, gdn_conv_w, gdn_a_log,
           gdn_dt_bias, gdn_norm_w, gdn_proj, conf_dw_w, conf_dw_b, conf_ln_w, conf_ln_b, conf_proj,
           sgu_ln_w, sgu_ln_b, sgu_w, sgu_b, sgu_proj, w_out, norm_mlp_w, mlp_w1, mlp_w2, norm_final_w):
    depth = w_in.shape[0]
    layers = [_prep_layer(l, norm_mix_w, w_in, pool_w, pool_scale, pool_proj, gdn_conv_w, gdn_a_log,
                          gdn_dt_bias, gdn_norm_w, gdn_proj, conf_dw_w, conf_dw_b, conf_ln_w, conf_ln_b,
                          conf_proj, sgu_ln_w, sgu_ln_b, sgu_w, sgu_b, sgu_proj, w_out, norm_mlp_w,
                          mlp_w1, mlp_w2) for l in range(depth)]
    y_prompt = _encoder(x_prompt, layers, norm_final_w)
    y_sample = _encoder(x_sample, layers, norm_final_w)
    return (y_prompt, y_sample)
```

```python
import functools

import jax
import jax.numpy as jnp
from jax import lax
from jax.experimental import pallas as pl
from jax.experimental.pallas import tpu as pltpu

F32 = jnp.float32
BF16 = jnp.bfloat16

D_MODEL = 2048
N_GROUPS = 4
GROUP_DIM = 128
MIX_W = N_GROUPS * GROUP_DIM
POOL_WINDOWS = (2, 4, 8, 16)
GDN_HEADS = 4
GDN_DK = 128
GDN_CONV = 4
GDN_CHUNK = 64
CONF_WIDTH = 31
SGU_CHUNK = 128
D_FF = 4 * D_MODEL
N_BRANCH = 4
EPS = 1e-6

LANE = 128
SUBLANE = 8
HALO = 16
GDN_BLOCK = 2 * GDN_CHUNK
GDN_STEP_BLOCKS = 2

OFF_QKV = 0
OFF_XA = 3 * MIX_W
OFF_XC = 4 * MIX_W
OFF_XD = 6 * MIX_W
OFF_XZ = 8 * MIX_W
OFF_AB = 9 * MIX_W
D_MIXP = OFF_AB + LANE
MIX_SECTIONS = ((OFF_QKV, 3 * MIX_W), (OFF_XA, MIX_W), (OFF_XC, 2 * MIX_W), (OFF_XD, 2 * MIX_W),
                (OFF_XZ, MIX_W), (OFF_AB, LANE))

VMEM_LIMIT = 60 * 1024 * 1024


def _cparams(sem):
    return pltpu.CompilerParams(dimension_semantics=sem, vmem_limit_bytes=VMEM_LIMIT)


def _sigmoid(x):
    return 1.0 / (1.0 + jnp.exp(-x))


def _bdot(a, b):
    return jnp.dot(a.astype(BF16), b.astype(BF16), preferred_element_type=F32)


def _chunk_masks(reverse):
    n = GDN_BLOCK
    row = lax.broadcasted_iota(jnp.int32, (n, n), 0)
    col = lax.broadcasted_iota(jnp.int32, (n, n), 1)
    same = (row // GDN_CHUNK) == (col // GDN_CHUNK)
    if reverse:
        return same & (col >= row), same & (col > row)
    return same & (col <= row), same & (col < row)


def _inproj_kernel(x_ref, nw_ref, w_ref, p_ref, h_ref, *, row_chunk):
    for r in range(x_ref.shape[0] // row_chunk):
        rows = pl.ds(r * row_chunk, row_chunk)
        x = x_ref[rows, :]
        ms = jnp.mean(x * x, axis=-1, keepdims=True)
        h = ((x * lax.rsqrt(ms + EPS)) * nw_ref[...]).astype(BF16)
        h_ref[rows, :] = h
        for off, width in MIX_SECTIONS:
            p_ref[rows, off:off + width] = jnp.dot(h, w_ref[:, off:off + width], preferred_element_type=F32)


def _in_proj(x2d, norm_w, w_mix, tm=512, row_chunk=256):
    t = x2d.shape[0]
    return pl.pallas_call(
        functools.partial(_inproj_kernel, row_chunk=row_chunk),
        grid=(t // tm,),
        in_specs=[
            pl.BlockSpec((tm, D_MODEL), lambda i: (i, 0)),
            pl.BlockSpec((1, D_MODEL), lambda i: (0, 0)),
            pl.BlockSpec((D_MODEL, D_MIXP), lambda i: (0, 0), pipeline_mode=pl.Buffered(1)),
        ],
        out_specs=[
            pl.BlockSpec((tm, D_MIXP), lambda i: (i, 0)),
            pl.BlockSpec((tm, D_MODEL), lambda i: (i, 0)),
        ],
        out_shape=[
            jax.ShapeDtypeStruct((t, D_MIXP), F32),
            jax.ShapeDtypeStruct((t, D_MODEL), BF16),
        ],
        compiler_params=_cparams(("parallel",)),
        name="in_proj",
    )(x2d, norm_w, w_mix)


def _layernorm(x, w, b):
    mu = jnp.mean(x, axis=-1, keepdims=True)
    xc = x - mu
    var = jnp.mean(xc * xc, axis=-1, keepdims=True)
    return xc * lax.rsqrt(var + EPS) * w + b


def _mix_kernel(qkv_m, qkv_p, qkv_n, xa_m, xa_p, xa_n, xc_m, xc_p, xc_n, xd_m, ab_m,
                pool_w, pool_scale, gconv_w, ab_a, ab_dtb, cdw_w, cdw_b, cln_w, cln_b,
                sln_w, sln_b, sgu_w, sgu_bias,
                ma_ref, mc_ref, md_ref, qkv_ref, abg_ref, abgt_ref,
                extq, exta, extc, rotc, *, ts, seq_len):
    tiles_per_seq = seq_len // ts
    s = pl.program_id(0) % tiles_per_seq
    left_ok = s > 0
    right_ok = s < tiles_per_seq - 1

    def fill(ext, main, prev, nxt, fn):
        ext[0:HALO, :] = jnp.where(left_ok, fn(prev[...]), 0.0)
        ext[HALO:HALO + ts, :] = fn(main[...])
        ext[HALO + ts:HALO + ts + HALO, :] = jnp.where(right_ok, fn(nxt[...]), 0.0)

    def glu(xc):
        return xc[:, :MIX_W] * _sigmoid(xc[:, MIX_W:])

    fill(extq, qkv_m, qkv_p, qkv_n, lambda x: x)
    fill(exta, xa_m, xa_p, xa_n, lambda x: x)
    fill(extc, xc_m, xc_p, xc_n, glu)

    pos = s * ts + lax.broadcasted_iota(jnp.int32, (ts, 1), 0)
    for g, win in enumerate(POOL_WINDOWS):
        half = win // 2
        cols = slice(g * GROUP_DIM, (g + 1) * GROUP_DIM)
        acc = exta[pl.ds(HALO - half, ts), cols]
        for d in range(-half + 1, half):
            acc = acc + exta[pl.ds(HALO + d, ts), cols]
        lo = jnp.maximum(pos - half, 0)
        hi = jnp.minimum(pos - half + win, seq_len)
        cnt = (hi - lo).astype(F32)
        xg = exta[pl.ds(HALO, ts), cols]
        y = _bdot(acc / cnt - xg, pool_w[g])
        ma_ref[:, cols] = (y * pool_scale[:, cols]).astype(BF16)

    rot_rows = rotc.shape[1]
    for r in range(1, SUBLANE):
        rotc[r - 1] = extc[pl.ds(r, rot_rows), :]
    pad_l = (CONF_WIDTH - 1) // 2
    acc = None
    for kk in range(CONF_WIDTH):
        q8, r = divmod(HALO - pad_l + kk, SUBLANE)
        src = extc[pl.ds(q8 * SUBLANE, ts), :] if r == 0 else rotc[r - 1, pl.ds(q8 * SUBLANE, ts), :]
        term = src * cdw_w[kk:kk + 1, :]
        acc = term if acc is None else acc + term
    hc = _layernorm(acc + cdw_b[...], cln_w[...], cln_b[...])
    mc_ref[...] = (hc * _sigmoid(hc)).astype(BF16)

    xd = xd_m[...]
    gd = jax.nn.gelu(xd, approximate=True)
    u = gd[:, :MIX_W]
    vn = _layernorm(gd[:, MIX_W:], sln_w[...], sln_b[...]).astype(BF16)
    for c in range(ts // SGU_CHUNK):
        rows = slice(c * SGU_CHUNK, (c + 1) * SGU_CHUNK)
        for g in range(N_GROUPS):
            cols = slice(g * GROUP_DIM, (g + 1) * GROUP_DIM)
            sv = jnp.dot(sgu_w[g], vn[rows, cols], preferred_element_type=F32) + sgu_bias[:, cols]
            md_ref[rows, cols] = (u[rows, cols] * sv).astype(BF16)

    gpad_l = (GDN_CONV - 1) // 2
    acc = extq[pl.ds(HALO - gpad_l, ts), :] * gconv_w[0:1, :]
    for kk in range(1, GDN_CONV):
        acc = acc + extq[pl.ds(HALO - gpad_l + kk, ts), :] * gconv_w[kk:kk + 1, :]
    qkv = acc * _sigmoid(acc)
    for hd in range(GDN_HEADS):
        qc = slice(hd * GDN_DK, (hd + 1) * GDN_DK)
        kc = slice(MIX_W + hd * GDN_DK, MIX_W + (hd + 1) * GDN_DK)
        qh = qkv[:, qc]
        kh = qkv[:, kc]
        qkv_ref[:, qc] = qh * lax.rsqrt(jnp.sum(qh * qh, axis=-1, keepdims=True) + EPS) * (GDN_DK ** -0.5)
        qkv_ref[:, kc] = kh * lax.rsqrt(jnp.sum(kh * kh, axis=-1, keepdims=True) + EPS)
    qkv_ref[:, 2 * MIX_W:] = qkv[:, 2 * MIX_W:]

    ab = ab_m[...]
    col = lax.broadcasted_iota(jnp.int32, (GDN_BLOCK, LANE), 1)
    xs = ab + ab_dtb[...]
    softplus = jnp.maximum(xs, 0.0) + jnp.log(1.0 + jnp.exp(-jnp.abs(xs)))
    gate = ab_a[...] * softplus
    beta = _sigmoid(ab)
    tri_f = _chunk_masks(False)[0].astype(F32)
    tri_b = _chunk_masks(True)[0].astype(F32)
    hi_dot = functools.partial(jnp.dot, precision=lax.Precision.HIGHEST, preferred_element_type=F32)
    for blk in range(ts // GDN_BLOCK):
        rows = slice(blk * GDN_BLOCK, (blk + 1) * GDN_BLOCK)
        gc_f = hi_dot(tri_f, gate[rows])
        gc_b = hi_dot(tri_b, gate[rows])
        res = jnp.where(col < 2 * GDN_HEADS, beta[rows], jnp.where(col < 3 * GDN_HEADS, gc_f, gc_b))
        abg_ref[rows, :] = res
        abgt_ref[:, rows] = res.T


def _mix(p, lw, seq_len, ts=512):
    t = p.shape[0]
    hb = ts // HALO
    n_hb = t // HALO

    def main(width, off):
        return pl.BlockSpec((ts, width), lambda i: (i, off // width))

    def prev(width, off):
        return pl.BlockSpec((HALO, width), lambda i: (jnp.maximum(i * hb - 1, 0), off // width))

    def nxt(width, off):
        return pl.BlockSpec((HALO, width), lambda i: (jnp.minimum((i + 1) * hb, n_hb - 1), off // width))

    def full(a):
        nd = a.ndim
        return pl.BlockSpec(a.shape, lambda i: (0,) * nd)

    params = (lw["pool_w"], lw["pool_scale"], lw["gconv_w"], lw["ab_a"], lw["ab_dtb"], lw["cdw_w"],
              lw["cdw_b"], lw["cln_w"], lw["cln_b"], lw["sln_w"], lw["sln_b"], lw["sgu_w"], lw["sgu_bias"])
    in_specs = [
        main(3 * MIX_W, OFF_QKV), prev(3 * MIX_W, OFF_QKV), nxt(3 * MIX_W, OFF_QKV),
        main(MIX_W, OFF_XA), prev(MIX_W, OFF_XA), nxt(MIX_W, OFF_XA),
        main(2 * MIX_W, OFF_XC), prev(2 * MIX_W, OFF_XC), nxt(2 * MIX_W, OFF_XC),
        main(2 * MIX_W, OFF_XD),
        main(LANE, OFF_AB),
    ] + [full(a) for a in params]
    tok = lambda w: pl.BlockSpec((ts, w), lambda i: (i, 0))
    out_specs = [tok(MIX_W)] * 3 + [tok(3 * MIX_W), tok(LANE), pl.BlockSpec((LANE, ts), lambda i: (0, i))]
    out_shape = ([jax.ShapeDtypeStruct((t, MIX_W), BF16)] * 3 + [jax.ShapeDtypeStruct((t, 3 * MIX_W), F32)]
                 + [jax.ShapeDtypeStruct((t, LANE), F32), jax.ShapeDtypeStruct((LANE, t), F32)])
    return pl.pallas_call(
        functools.partial(_mix_kernel, ts=ts, seq_len=seq_len),
        grid=(t // ts,),
        in_specs=in_specs,
        out_specs=out_specs,
        out_shape=out_shape,
        scratch_shapes=[
            pltpu.VMEM((ts + 2 * HALO, 3 * MIX_W), F32),
            pltpu.VMEM((ts + 2 * HALO, MIX_W), F32),
            pltpu.VMEM((ts + 2 * HALO, MIX_W), F32),
            pltpu.VMEM((SUBLANE - 1, ts + 2 * HALO - SUBLANE, MIX_W), F32),
        ],
        compiler_params=_cparams(("parallel",)),
        name="mix",
    )(p, p, p, p, p, p, p, p, p, p, p, *params)


def _gdn_kernel(qkvf, abf, abtf, qkvb, abb, abtb, of_ref, ob_ref, s_ref):
    @pl.when(pl.program_id(1) == 0)
    def _():
        s_ref[...] = jnp.zeros_like(s_ref)

    n = GDN_BLOCK
    rowv = lax.broadcasted_iota(jnp.int32, (n, 1), 0)
    qkv_refs, ab_refs, abt_refs, o_refs = (qkvf, qkvb), (abf, abb), (abtf, abtb), (of_ref, ob_ref)
    masks = (_chunk_masks(False), _chunk_masks(True))
    last = ((GDN_CHUNK - 1, n - 1), (0, GDN_CHUNK))
    order = ((0, 1), (1, 0))

    items = [(d, hd) for d in range(2) for hd in range(GDN_HEADS)]
    idx = range(len(items))
    hcols = [slice(hd * GDN_DK, (hd + 1) * GDN_DK) for _, hd in items]
    cgs = [2 * GDN_HEADS + d * GDN_HEADS + hd for d, hd in items]
    state = [s_ref[d, hd] for d, hd in items]
    for sub in range(GDN_STEP_BLOCKS):
        rows = (pl.ds(sub * n, n), pl.ds((GDN_STEP_BLOCKS - 1 - sub) * n, n))
        ab = [ab_refs[d][rows[d], :] for d in range(2)]
        abt = [abt_refs[d][:, rows[d]] for d in range(2)]
        beta = [ab[d][:, d * GDN_HEADS + hd:d * GDN_HEADS + hd + 1] for d, hd in items]
        gcc = [ab[d][:, cg:cg + 1] for (d, _), cg in zip(items, cgs)]
        decay = [jnp.exp(jnp.where(masks[d][0], gcc[i] - abt[d][cgs[i]:cgs[i] + 1, :], -jnp.inf))
                 for i, (d, _) in enumerate(items)]
        sec = lambda d, hd, k: qkv_refs[d][rows[d], k * MIX_W + hd * GDN_DK:k * MIX_W + (hd + 1) * GDN_DK]
        qh = [sec(d, hd, 0) for d, hd in items]
        kh = [sec(d, hd, 1) for d, hd in items]
        vh = [sec(d, hd, 2) for d, hd in items]
        kh16 = [x.astype(BF16) for x in kh]
        dn = (((1,), (1,)), ((), ()))
        kq = [lax.dot_general(jnp.concatenate([kh16[i], qh[i].astype(BF16)], axis=0), kh16[i], dn,
                              preferred_element_type=F32) for i in idx]
        gram = [x[:n] for x in kq]
        qkm = [kq[i][n:] * decay[i] for i in idx]
        toff = [jnp.where(masks[d][1], -(beta[i] * gram[i] * decay[i]), 0.0) for i, (d, _) in enumerate(items)]
        p16 = [x.astype(BF16) for x in toff]
        for _ in range(5):
            p = [jnp.dot(p16[i], p16[i], preferred_element_type=F32) for i in idx]
            p16 = [x.astype(BF16) for x in p]
            toff = [toff[i] + p[i] + jnp.dot(toff[i].astype(BF16), p16[i], preferred_element_type=F32)
                    for i in idx]
        eg = [jnp.exp(g) for g in gcc]
        rhs = [jnp.concatenate([vh[i] * beta[i], kh[i] * (beta[i] * eg[i])], axis=1) for i in idx]
        uw = [rhs[i] + _bdot(toff[i], rhs[i]) for i in idx]
        qg = [qh[i] * eg[i] for i in idx]
        gl = [[ab[d][r:r + 1, cgs[i]:cgs[i] + 1] for r in last[d]] for i, (d, _) in enumerate(items)]
        kg_t = [(kh[i] * jnp.exp(jnp.where(rowv < GDN_CHUNK, gl[i][0], gl[i][1]) - gcc[i])).T for i in idx]
        outs = [[None, None] for _ in idx]
        for step in range(2):
            cs = [order[d][step] for d, _ in items]
            rs = [slice(c * GDN_CHUNK, (c + 1) * GDN_CHUNK) for c in cs]
            ws = [_bdot(jnp.concatenate([uw[i][rs[i], GDN_DK:], qg[i][rs[i]]], axis=0), state[i]) for i in idx]
            v_new = [uw[i][rs[i], :GDN_DK] - ws[i][:GDN_CHUNK] for i in idx]
            for i in idx:
                outs[i][cs[i]] = ws[i][GDN_CHUNK:] + _bdot(qkm[i][rs[i], rs[i]], v_new[i])
            state = [state[i] * jnp.exp(gl[i][cs[i]]) + _bdot(kg_t[i][:, rs[i]], v_new[i]) for i in idx]
        for i, (d, hd) in enumerate(items):
            o_refs[d][rows[d], hcols[i]] = jnp.concatenate(outs[i], axis=0)
    for i, (d, hd) in enumerate(items):
        s_ref[d, hd] = state[i]


def _gdn(qkv, abg, abgt, seq_len):
    t = qkv.shape[0]
    n_seq = t // seq_len
    step_rows = GDN_STEP_BLOCKS * GDN_BLOCK
    sps = seq_len // step_rows
    fwd = lambda b, i: b * sps + i
    bwd = lambda b, i: b * sps + sps - 1 - i

    def tok(blk, width):
        return pl.BlockSpec((step_rows, width), lambda b, i: (blk(b, i), 0))

    def specs(blk):
        return [tok(blk, 3 * MIX_W), tok(blk, LANE), pl.BlockSpec((LANE, step_rows), lambda b, i: (0, blk(b, i)))]

    out_sd = jax.ShapeDtypeStruct((t, MIX_W), F32)
    return pl.pallas_call(
        _gdn_kernel,
        grid=(n_seq, sps),
        in_specs=specs(fwd) + specs(bwd),
        out_specs=[tok(fwd, MIX_W), tok(bwd, MIX_W)],
        out_shape=[out_sd, out_sd],
        scratch_shapes=[pltpu.VMEM((2, GDN_HEADS, GDN_DK, GDN_DK), F32)],
        compiler_params=_cparams(("arbitrary", "arbitrary")),
        name="gdn",
    )(qkv, abg, abgt, qkv, abg, abgt)


def _merge_kernel(h_ref, ma, of_ref, ob_ref, z_ref, gnw_ref, mc, md, wg0, wg1, wg2, wg3, pa, pb, pc, pd, o_ref,
                  mb_ref):
    @pl.when(pl.program_id(1) == 0)
    def _():
        for hd in range(GDN_HEADS):
            cols = slice(hd * GDN_DK, (hd + 1) * GDN_DK)
            o = of_ref[:, cols] + ob_ref[:, cols]
            o = o * lax.rsqrt(jnp.mean(o * o, axis=-1, keepdims=True) + EPS) * gnw_ref[...]
            z = z_ref[:, cols]
            mb_ref[:, cols] = (o * (z * _sigmoid(z))).astype(BF16)

    h = h_ref[...]
    acc = None
    for m_ref, wg, pr in ((ma, wg0, pa), (mb_ref, wg1, pb), (mc, wg2, pc), (md, wg3, pd)):
        gate = _sigmoid(jnp.dot(h, wg[...], preferred_element_type=F32))
        y = jnp.dot(m_ref[...], pr[...], preferred_element_type=F32)
        acc = gate * y if acc is None else acc + gate * y
    o_ref[...] = acc.astype(BF16)


def _merge(h, m_a, o_f, o_b, p, gdn_norm_w, m_c, m_d, w_gate, projs, tm=512, tn=512):
    t = h.shape[0]
    nj = D_MODEL // tn
    tokm = pl.BlockSpec((tm, MIX_W), lambda i, j: (i, 0))
    in_specs = [pl.BlockSpec((tm, D_MODEL), lambda i, j: (i, 0)), tokm, tokm, tokm,
                pl.BlockSpec((tm, MIX_W), lambda i, j: (i, OFF_XZ // MIX_W)),
                pl.BlockSpec((1, GDN_DK), lambda i, j: (0, 0)), tokm, tokm]
    in_specs += [pl.BlockSpec((D_MODEL, tn), functools.partial(lambda i, j, b: (0, b * nj + j), b=b))
                 for b in range(N_BRANCH)]
    in_specs += [pl.BlockSpec((MIX_W, tn), lambda i, j: (0, j))] * 4
    return pl.pallas_call(
        _merge_kernel,
        grid=(t // tm, nj),
        in_specs=in_specs,
        out_specs=pl.BlockSpec((tm, tn), lambda i, j: (i, j)),
        out_shape=jax.ShapeDtypeStruct((t, D_MODEL), BF16),
        scratch_shapes=[pltpu.VMEM((tm, MIX_W), BF16)],
        compiler_params=_cparams(("parallel", "arbitrary")),
        name="merge",
    )(h, m_a, o_f, o_b, p, gdn_norm_w, m_c, m_d, w_gate, w_gate, w_gate, w_gate, *projs)


def _outproj_kernel(x_ref, m_ref, w_ref, nw_ref, x1_ref, h2_ref):
    x1 = x_ref[...] + jnp.dot(m_ref[...], w_ref[...], preferred_element_type=F32)
    x1_ref[...] = x1
    ms = jnp.mean(x1 * x1, axis=-1, keepdims=True)
    h2_ref[...] = ((x1 * lax.rsqrt(ms + EPS)) * nw_ref[...]).astype(BF16)


def _out_proj(x2d, merged, w_out, norm_w, tm=512):
    t = x2d.shape[0]
    tok = pl.BlockSpec((tm, D_MODEL), lambda i: (i, 0))
    return pl.pallas_call(
        _outproj_kernel,
        grid=(t // tm,),
        in_specs=[tok, tok, pl.BlockSpec((D_MODEL, D_MODEL), lambda i: (0, 0)),
                  pl.BlockSpec((1, D_MODEL), lambda i: (0, 0))],
        out_specs=[tok, tok],
        out_shape=[jax.ShapeDtypeStruct((t, D_MODEL), F32), jax.ShapeDtypeStruct((t, D_MODEL), BF16)],
        compiler_params=_cparams(("parallel",)),
        name="out_proj",
    )(x2d, merged, w_out, norm_w)


def _mlp_kernel(x1_ref, h2_ref, w1_ref, w2_ref, nw_ref, o_ref, *, final, col_chunk):
    f = pl.program_id(1)

    @pl.when(f == 0)
    def _():
        o_ref[...] = x1_ref[...]

    a = jnp.maximum(jnp.dot(h2_ref[...], w1_ref[...], preferred_element_type=F32), 0.0)
    a = (a * a).astype(BF16)
    for c in range(0, D_MODEL, col_chunk):
        o_ref[:, c:c + col_chunk] += jnp.dot(a, w2_ref[:, c:c + col_chunk], preferred_element_type=F32)

    if final:
        @pl.when(f == pl.num_programs(1) - 1)
        def _():
            x2 = o_ref[...]
            ms = jnp.mean(x2 * x2, axis=-1, keepdims=True)
            o_ref[...] = (x2 * lax.rsqrt(ms + EPS)) * nw_ref[...]


def _mlp(x1, h2, w1, w2, norm_w, final, tm=512, tf=1024, col_chunk=512):
    t = x1.shape[0]
    tok = pl.BlockSpec((tm, D_MODEL), lambda i, f: (i, 0))
    return pl.pallas_call(
        functools.partial(_mlp_kernel, final=final, col_chunk=col_chunk),
        grid=(t // tm, D_FF // tf),
        in_specs=[tok, tok,
                  pl.BlockSpec((D_MODEL, tf), lambda i, f: (0, f)),
                  pl.BlockSpec((tf, D_MODEL), lambda i, f: (f, 0)),
                  pl.BlockSpec((1, D_MODEL), lambda i, f: (0, 0))],
        out_specs=tok,
        out_shape=jax.ShapeDtypeStruct((t, D_MODEL), F32),
        compiler_params=_cparams(("parallel", "arbitrary")),
        name="mlp",
    )(x1, h2, w1, w2, norm_w)


def _prep_layer(l, norm_mix_w, w_in, pool_w, pool_scale, pool_proj, gdn_conv_w, gdn_a_log, gdn_dt_bias,
                gdn_norm_w, gdn_proj, conf_dw_w, conf_dw_b, conf_ln_w, conf_ln_b, conf_proj, sgu_ln_w,
                sgu_ln_b, sgu_w, sgu_b, sgu_proj, w_out, norm_mlp_w, mlp_w1, mlp_w2):
    w = w_in[l]
    o_a, o_qkv, o_z, o_ab, o_c, o_d, o_g = 0, MIX_W, 4 * MIX_W, 5 * MIX_W, 5 * MIX_W + 16, 7 * MIX_W + 16, 9 * MIX_W + 16
    w_mix = jnp.concatenate([
        w[:, o_qkv:o_z], w[:, o_a:o_qkv], w[:, o_c:o_d], w[:, o_d:o_g], w[:, o_z:o_ab], w[:, o_ab:o_c],
        jnp.zeros((D_MODEL, D_MIXP - OFF_AB - 16), F32)], axis=1).astype(BF16)
    row = lambda a: a.reshape(1, -1).astype(F32)
    pad_row = lambda a: jnp.zeros((1, LANE), F32).at[0, 2 * GDN_HEADS:4 * GDN_HEADS].set(a.reshape(-1))
    return dict(
        norm_mix_w=row(norm_mix_w[l]),
        w_mix=w_mix,
        w_gate=w[:, o_g:].astype(BF16),
        pool_w=pool_w[l].astype(BF16),
        pool_scale=row(pool_scale[l]),
        gconv_w=gdn_conv_w[l].astype(F32),
        ab_a=pad_row(-jnp.exp(gdn_a_log[l].astype(F32))),
        ab_dtb=pad_row(gdn_dt_bias[l].astype(F32)),
        gdn_norm_w=row(gdn_norm_w[l]),
        cdw_w=conf_dw_w[l].astype(F32),
        cdw_b=row(conf_dw_b[l]),
        cln_w=row(conf_ln_w[l]),
        cln_b=row(conf_ln_b[l]),
        sln_w=row(sgu_ln_w[l]),
        sln_b=row(sgu_ln_b[l]),
        sgu_w=sgu_w[l].astype(BF16),
        sgu_bias=jnp.repeat(sgu_b[l].astype(F32).T, GROUP_DIM, axis=1),
        projs=tuple(pr[l].astype(BF16) for pr in (pool_proj, gdn_proj, conf_proj, sgu_proj)),
        w_out=w_out[l].astype(BF16),
        norm_mlp_w=row(norm_mlp_w[l]),
        w1=mlp_w1[l].astype(BF16),
        w2=mlp_w2[l].astype(BF16),
    )


def _encoder(x, layers, norm_final_w):
    b, s, _ = x.shape
    x2d = x.reshape(b * s, D_MODEL)
    nf = norm_final_w.reshape(1, -1).astype(F32)
    for li, lw in enumerate(layers):
        p, h = _in_proj(x2d, lw["norm_mix_w"], lw["w_mix"])
        m_a, m_c, m_d, qkv, abg, abgt = _mix(p, lw, s)
        o_f, o_b = _gdn(qkv, abg, abgt, s)
        merged = _merge(h, m_a, o_f, o_b, p, lw["gdn_norm_w"], m_c, m_d, lw["w_gate"], lw["projs"])
        x1, h2 = _out_proj(x2d, merged, lw["w_out"], lw["norm_mlp_w"])
        x2d = _mlp(x1, h2, lw["w1"], lw["w2"], nf, final=(li == len(layers) - 1))
    return x2d.reshape(b, s, D_MODEL)


def kernel(x_prompt, x_sample, norm_mix_w, w_in, pool_w, pool_scale, pool_proj, gdn_conv_w, gdn_a_log,
           gdn_dt_bias, gdn_norm_w, gdn_proj, conf_dw_w, conf_dw_b, conf_ln_w, conf_ln_b, conf_proj,
           sgu_ln_w, sgu_ln_b, sgu_w, sgu_b, sgu_proj, w_out, norm_mlp_w, mlp_w1, mlp_w2, norm_final_w):
    depth = w_in.shape[0]
    layers = [_prep_layer(l, norm_mix_w, w_in, pool_w, pool_scale, pool_proj, gdn_conv_w, gdn_a_log,
                          gdn_dt_bias, gdn_norm_w, gdn_proj, conf_dw_w, conf_dw_b, conf_ln_w, conf_ln_b,
                          conf_proj, sgu_ln_w, sgu_ln_b, sgu_w, sgu_b, sgu_proj, w_out, norm_mlp_w,
                          mlp_w1, mlp_w2) for l in range(depth)]
    y_prompt = _encoder(x_prompt, layers, norm_final_w)
    y_sample = _encoder(x_sample, layers, norm_final_w)
    return (y_prompt, y_sample)
```

```python
import functools

import jax
import jax.numpy as jnp
from jax import lax
from jax.experimental import pallas as pl
from jax.experimental.pallas import tpu as pltpu

F32 = jnp.float32
BF16 = jnp.bfloat16

D_MODEL = 2048
N_GROUPS = 4
GROUP_DIM = 128
MIX_W = N_GROUPS * GROUP_DIM
POOL_WINDOWS = (2, 4, 8, 16)
GDN_HEADS = 4
GDN_DK = 128
GDN_CONV = 4
GDN_CHUNK = 64
CONF_WIDTH = 31
SGU_CHUNK = 128
D_FF = 4 * D_MODEL
N_BRANCH = 4
EPS = 1e-6

LANE = 128
SUBLANE = 8
HALO = 16
GDN_BLOCK = 2 * GDN_CHUNK
GDN_STEP_BLOCKS = 2

OFF_QKV = 0
OFF_XA = 3 * MIX_W
OFF_XC = 4 * MIX_W
OFF_XD = 6 * MIX_W
OFF_XZ = 8 * MIX_W
OFF_AB = 9 * MIX_W
D_MIXP = OFF_AB + LANE
MIX_SECTIONS = ((OFF_QKV, 3 * MIX_W), (OFF_XA, MIX_W), (OFF_XC, 2 * MIX_W), (OFF_XD, 2 * MIX_W),
                (OFF_XZ, MIX_W), (OFF_AB, LANE))

VMEM_LIMIT = 60 * 1024 * 1024


def _cparams(sem):
    return pltpu.CompilerParams(dimension_semantics=sem, vmem_limit_bytes=VMEM_LIMIT)


def _sigmoid(x):
    return 1.0 / (1.0 + jnp.exp(-x))


def _bdot(a, b):
    return jnp.dot(a.astype(BF16), b.astype(BF16), preferred_element_type=F32)


def _chunk_masks(reverse):
    n = GDN_BLOCK
    row = lax.broadcasted_iota(jnp.int32, (n, n), 0)
    col = lax.broadcasted_iota(jnp.int32, (n, n), 1)
    same = (row // GDN_CHUNK) == (col // GDN_CHUNK)
    if reverse:
        return same & (col >= row), same & (col > row)
    return same & (col <= row), same & (col < row)


def _inproj_kernel(x_ref, nw_ref, w_ref, p_ref, h_ref, *, row_chunk):
    for r in range(x_ref.shape[0] // row_chunk):
        rows = pl.ds(r * row_chunk, row_chunk)
        x = x_ref[rows, :]
        ms = jnp.mean(x * x, axis=-1, keepdims=True)
        h = ((x * lax.rsqrt(ms + EPS)) * nw_ref[...]).astype(BF16)
        h_ref[rows, :] = h
        for off, width in MIX_SECTIONS:
            p_ref[rows, off:off + width] = jnp.dot(h, w_ref[:, off:off + width], preferred_element_type=F32)


def _in_proj(x2d, norm_w, w_mix, tm=512, row_chunk=256):
    t = x2d.shape[0]
    return pl.pallas_call(
        functools.partial(_inproj_kernel, row_chunk=row_chunk),
        grid=(t // tm,),
        in_specs=[
            pl.BlockSpec((tm, D_MODEL), lambda i: (i, 0)),
            pl.BlockSpec((1, D_MODEL), lambda i: (0, 0)),
            pl.BlockSpec((D_MODEL, D_MIXP), lambda i: (0, 0), pipeline_mode=pl.Buffered(1)),
        ],
        out_specs=[
            pl.BlockSpec((tm, D_MIXP), lambda i: (i, 0)),
            pl.BlockSpec((tm, D_MODEL), lambda i: (i, 0)),
        ],
        out_shape=[
            jax.ShapeDtypeStruct((t, D_MIXP), F32),
            jax.ShapeDtypeStruct((t, D_MODEL), BF16),
        ],
        compiler_params=_cparams(("parallel",)),
        name="in_proj",
    )(x2d, norm_w, w_mix)


def _layernorm(x, w, b):
    mu = jnp.mean(x, axis=-1, keepdims=True)
    xc = x - mu
    var = jnp.mean(xc * xc, axis=-1, keepdims=True)
    return xc * lax.rsqrt(var + EPS) * w + b


def _mix_kernel(qkv_m, qkv_p, qkv_n, xa_m, xa_p, xa_n, xc_m, xc_p, xc_n, xd_m, ab_m,
                pool_w, pool_scale, gconv_w, ab_a, ab_dtb, cdw_w, cdw_b, cln_w, cln_b,
                sln_w, sln_b, sgu_w, sgu_bias,
                ma_ref, mc_ref, md_ref, qkv_ref, abg_ref, abgt_ref,
                extq, exta, extc, rotc, *, ts, seq_len):
    tiles_per_seq = seq_len // ts
    s = pl.program_id(0) % tiles_per_seq
    left_ok = s > 0
    right_ok = s < tiles_per_seq - 1

    def fill(ext, main, prev, nxt, fn):
        ext[0:HALO, :] = jnp.where(left_ok, fn(prev[...]), 0.0)
        ext[HALO:HALO + ts, :] = fn(main[...])
        ext[HALO + ts:HALO + ts + HALO, :] = jnp.where(right_ok, fn(nxt[...]), 0.0)

    def glu(xc):
        return xc[:, :MIX_W] * _sigmoid(xc[:, MIX_W:])

    fill(extq, qkv_m, qkv_p, qkv_n, lambda x: x)
    fill(exta, xa_m, xa_p, xa_n, lambda x: x)
    fill(extc, xc_m, xc_p, xc_n, glu)

    pos = s * ts + lax.broadcasted_iota(jnp.int32, (ts, 1), 0)
    for g, win in enumerate(POOL_WINDOWS):
        half = win // 2
        cols = slice(g * GROUP_DIM, (g + 1) * GROUP_DIM)
        acc = exta[pl.ds(HALO - half, ts), cols]
        for d in range(-half + 1, half):
            acc = acc + exta[pl.ds(HALO + d, ts), cols]
        lo = jnp.maximum(pos - half, 0)
        hi = jnp.minimum(pos - half + win, seq_len)
        cnt = (hi - lo).astype(F32)
        xg = exta[pl.ds(HALO, ts), cols]
        y = _bdot(acc / cnt - xg, pool_w[g])
        ma_ref[:, cols] = (y * pool_scale[:, cols]).astype(BF16)

    rot_rows = rotc.shape[1]
    for r in range(1, SUBLANE):
        rotc[r - 1] = extc[pl.ds(r, rot_rows), :]
    pad_l = (CONF_WIDTH - 1) // 2
    acc = None
    for kk in range(CONF_WIDTH):
        q8, r = divmod(HALO - pad_l + kk, SUBLANE)
        src = extc[pl.ds(q8 * SUBLANE, ts), :] if r == 0 else rotc[r - 1, pl.ds(q8 * SUBLANE, ts), :]
        term = src * cdw_w[kk:kk + 1, :]
        acc = term if acc is None else acc + term
    hc = _layernorm(acc + cdw_b[...], cln_w[...], cln_b[...])
    mc_ref[...] = (hc * _sigmoid(hc)).astype(BF16)

    xd = xd_m[...]
    gd = jax.nn.gelu(xd, approximate=True)
    u = gd[:, :MIX_W]
    vn = _layernorm(gd[:, MIX_W:], sln_w[...], sln_b[...]).astype(BF16)
    for c in range(ts // SGU_CHUNK):
        rows = slice(c * SGU_CHUNK, (c + 1) * SGU_CHUNK)
        for g in range(N_GROUPS):
            cols = slice(g * GROUP_DIM, (g + 1) * GROUP_DIM)
            sv = jnp.dot(sgu_w[g], vn[rows, cols], preferred_element_type=F32) + sgu_bias[:, cols]
            md_ref[rows, cols] = (u[rows, cols] * sv).astype(BF16)

    gpad_l = (GDN_CONV - 1) // 2
    acc = extq[pl.ds(HALO - gpad_l, ts), :] * gconv_w[0:1, :]
    for kk in range(1, GDN_CONV):
        acc = acc + extq[pl.ds(HALO - gpad_l + kk, ts), :] * gconv_w[kk:kk + 1, :]
    qkv = acc * _sigmoid(acc)
    for hd in range(GDN_HEADS):
        qc = slice(hd * GDN_DK, (hd + 1) * GDN_DK)
        kc = slice(MIX_W + hd * GDN_DK, MIX_W + (hd + 1) * GDN_DK)
        qh = qkv[:, qc]
        kh = qkv[:, kc]
        qkv_ref[:, qc] = qh * lax.rsqrt(jnp.sum(qh * qh, axis=-1, keepdims=True) + EPS) * (GDN_DK ** -0.5)
        qkv_ref[:, kc] = kh * lax.rsqrt(jnp.sum(kh * kh, axis=-1, keepdims=True) + EPS)
    qkv_ref[:, 2 * MIX_W:] = qkv[:, 2 * MIX_W:]

    ab = ab_m[...]
    col = lax.broadcasted_iota(jnp.int32, (GDN_BLOCK, LANE), 1)
    xs = ab + ab_dtb[...]
    softplus = jnp.maximum(xs, 0.0) + jnp.log(1.0 + jnp.exp(-jnp.abs(xs)))
    gate = ab_a[...] * softplus
    beta = _sigmoid(ab)
    tri_f = _chunk_masks(False)[0].astype(F32)
    tri_b = _chunk_masks(True)[0].astype(F32)
    hi_dot = functools.partial(jnp.dot, precision=lax.Precision.HIGHEST, preferred_element_type=F32)
    for blk in range(ts // GDN_BLOCK):
        rows = slice(blk * GDN_BLOCK, (blk + 1) * GDN_BLOCK)
        gc_f = hi_dot(tri_f, gate[rows])
        gc_b = hi_dot(tri_b, gate[rows])
        res = jnp.where(col < 2 * GDN_HEADS, beta[rows], jnp.where(col < 3 * GDN_HEADS, gc_f, gc_b))
        abg_ref[rows, :] = res
        abgt_ref[:, rows] = res.T


def _mix(p, lw, seq_len, ts=512):
    t = p.shape[0]
    hb = ts // HALO
    n_hb = t // HALO

    def main(width, off):
        return pl.BlockSpec((ts, width), lambda i: (i, off // width))

    def prev(width, off):
        return pl.BlockSpec((HALO, width), lambda i: (jnp.maximum(i * hb - 1, 0), off // width))

    def nxt(width, off):
        return pl.BlockSpec((HALO, width), lambda i: (jnp.minimum((i + 1) * hb, n_hb - 1), off // width))

    def full(a):
        nd = a.ndim
        return pl.BlockSpec(a.shape, lambda i: (0,) * nd)

    params = (lw["pool_w"], lw["pool_scale"], lw["gconv_w"], lw["ab_a"], lw["ab_dtb"], lw["cdw_w"],
              lw["cdw_b"], lw["cln_w"], lw["cln_b"], lw["sln_w"], lw["sln_b"], lw["sgu_w"], lw["sgu_bias"])
    in_specs = [
        main(3 * MIX_W, OFF_QKV), prev(3 * MIX_W, OFF_QKV), nxt(3 * MIX_W, OFF_QKV),
        main(MIX_W, OFF_XA), prev(MIX_W, OFF_XA), nxt(MIX_W, OFF_XA),
        main(2 * MIX_W, OFF_XC), prev(2 * MIX_W, OFF_XC), nxt(2 * MIX_W, OFF_XC),
        main(2 * MIX_W, OFF_XD),
        main(LANE, OFF_AB),
    ] + [full(a) for a in params]
    tok = lambda w: pl.BlockSpec((ts, w), lambda i: (i, 0))
    out_specs = [tok(MIX_W)] * 3 + [tok(3 * MIX_W), tok(LANE), pl.BlockSpec((LANE, ts), lambda i: (0, i))]
    out_shape = ([jax.ShapeDtypeStruct((t, MIX_W), BF16)] * 3 + [jax.ShapeDtypeStruct((t, 3 * MIX_W), F32)]
                 + [jax.ShapeDtypeStruct((t, LANE), F32), jax.ShapeDtypeStruct((LANE, t), F32)])
    return pl.pallas_call(
        functools.partial(_mix_kernel, ts=ts, seq_len=seq_len),
        grid=(t // ts,),
        in_specs=in_specs,
        out_specs=out_specs,
        out_shape=out_shape,
        scratch_shapes=[
            pltpu.VMEM((ts + 2 * HALO, 3 * MIX_W), F32),
            pltpu.VMEM((ts + 2 * HALO, MIX_W), F32),
            pltpu.VMEM((ts + 2 * HALO, MIX_W), F32),
            pltpu.VMEM((SUBLANE - 1, ts + 2 * HALO - SUBLANE, MIX_W), F32),
        ],
        compiler_params=_cparams(("parallel",)),
        name="mix",
    )(p, p, p, p, p, p, p, p, p, p, p, *params)


def _gdn_kernel(qkvf, abf, abtf, qkvb, abb, abtb, of_ref, ob_ref, s_ref):
    @pl.when(pl.program_id(1) == 0)
    def _():
        s_ref[...] = jnp.zeros_like(s_ref)

    n = GDN_BLOCK
    rowv = lax.broadcasted_iota(jnp.int32, (n, 1), 0)
    qkv_refs, ab_refs, abt_refs, o_refs = (qkvf, qkvb), (abf, abb), (abtf, abtb), (of_ref, ob_ref)
    masks = (_chunk_masks(False), _chunk_masks(True))
    last = ((GDN_CHUNK - 1, n - 1), (0, GDN_CHUNK))
    order = ((0, 1), (1, 0))

    items = [(d, hd) for d in range(2) for hd in range(GDN_HEADS)]
    idx = range(len(items))
    hcols = [slice(hd * GDN_DK, (hd + 1) * GDN_DK) for _, hd in items]
    cgs = [2 * GDN_HEADS + d * GDN_HEADS + hd for d, hd in items]
    state = [s_ref[d, hd] for d, hd in items]
    for sub in range(GDN_STEP_BLOCKS):
        rows = (pl.ds(sub * n, n), pl.ds((GDN_STEP_BLOCKS - 1 - sub) * n, n))
        ab = [ab_refs[d][rows[d], :] for d in range(2)]
        abt = [abt_refs[d][:, rows[d]] for d in range(2)]
        beta = [ab[d][:, d * GDN_HEADS + hd:d * GDN_HEADS + hd + 1] for d, hd in items]
        gcc = [ab[d][:, cg:cg + 1] for (d, _), cg in zip(items, cgs)]
        decay = [jnp.exp(jnp.where(masks[d][0], gcc[i] - abt[d][cgs[i]:cgs[i] + 1, :], -jnp.inf))
                 for i, (d, _) in enumerate(items)]
        sec = lambda d, hd, k: qkv_refs[d][rows[d], k * MIX_W + hd * GDN_DK:k * MIX_W + (hd + 1) * GDN_DK]
        qh = [sec(d, hd, 0) for d, hd in items]
        kh = [sec(d, hd, 1) for d, hd in items]
        vh = [sec(d, hd, 2) for d, hd in items]
        kh16 = [x.astype(BF16) for x in kh]
        dn = (((1,), (1,)), ((), ()))
        kq = [lax.dot_general(jnp.concatenate([kh16[i], qh[i].astype(BF16)], axis=0), kh16[i], dn,
                              preferred_element_type=F32) for i in idx]
        gram = [x[:n] for x in kq]
        qkm = [kq[i][n:] * decay[i] for i in idx]
        toff = [jnp.where(masks[d][1], -(beta[i] * gram[i] * decay[i]), 0.0) for i, (d, _) in enumerate(items)]
        p16 = [x.astype(BF16) for x in toff]
        for _ in range(5):
            p = [jnp.dot(p16[i], p16[i], preferred_element_type=F32) for i in idx]
            p16 = [x.astype(BF16) for x in p]
            toff = [toff[i] + p[i] + jnp.dot(toff[i].astype(BF16), p16[i], preferred_element_type=F32)
                    for i in idx]
        eg = [jnp.exp(g) for g in gcc]
        rhs = [jnp.concatenate([vh[i] * beta[i], kh[i] * (beta[i] * eg[i])], axis=1) for i in idx]
        uw = [rhs[i] + _bdot(toff[i], rhs[i]) for i in idx]
        qg = [qh[i] * eg[i] for i in idx]
        gl = [[ab[d][r:r + 1, cgs[i]:cgs[i] + 1] for r in last[d]] for i, (d, _) in enumerate(items)]
        kg_t = [(kh[i] * jnp.exp(jnp.where(rowv < GDN_CHUNK, gl[i][0], gl[i][1]) - gcc[i])).T for i in idx]
        outs = [[None, None] for _ in idx]
        for step in range(2):
            cs = [order[d][step] for d, _ in items]
            rs = [slice(c * GDN_CHUNK, (c + 1) * GDN_CHUNK) for c in cs]
            ws = [_bdot(jnp.concatenate([uw[i][rs[i], GDN_DK:], qg[i][rs[i]]], axis=0), state[i]) for i in idx]
            v_new = [uw[i][rs[i], :GDN_DK] - ws[i][:GDN_CHUNK] for i in idx]
            for i in idx:
                outs[i][cs[i]] = ws[i][GDN_CHUNK:] + _bdot(qkm[i][rs[i], rs[i]], v_new[i])
            state = [state[i] * jnp.exp(gl[i][cs[i]]) + _bdot(kg_t[i][:, rs[i]], v_new[i]) for i in idx]
        for i, (d, hd) in enumerate(items):
            o_refs[d][rows[d], hcols[i]] = jnp.concatenate(outs[i], axis=0)
    for i, (d, hd) in enumerate(items):
        s_ref[d, hd] = state[i]


def _gdn(qkv, abg, abgt, seq_len):
    t = qkv.shape[0]
    n_seq = t // seq_len
    step_rows = GDN_STEP_BLOCKS * GDN_BLOCK
    sps = seq_len // step_rows
    fwd = lambda b, i: b * sps + i
    bwd = lambda b, i: b * sps + sps - 1 - i

    def tok(blk, width):
        return pl.BlockSpec((step_rows, width), lambda b, i: (blk(b, i), 0))

    def specs(blk):
        return [tok(blk, 3 * MIX_W), tok(blk, LANE), pl.BlockSpec((LANE, step_rows), lambda b, i: (0, blk(b, i)))]

    out_sd = jax.ShapeDtypeStruct((t, MIX_W), F32)
    return pl.pallas_call(
        _gdn_kernel,
        grid=(n_seq, sps),
        in_specs=specs(fwd) + specs(bwd),
        out_specs=[tok(fwd, MIX_W), tok(bwd, MIX_W)],
        out_shape=[out_sd, out_sd],
        scratch_shapes=[pltpu.VMEM((2, GDN_HEADS, GDN_DK, GDN_DK), F32)],
        compiler_params=_cparams(("arbitrary", "arbitrary")),
        name="gdn",
    )(qkv, abg, abgt, qkv, abg, abgt)


def _merge_kernel(h_ref, ma, of_ref, ob_ref, z_ref, gnw_ref, mc, md, wg0, wg1, wg2, wg3, pa, pb, pc, pd, o_ref,
                  mb_ref):
    @pl.when(pl.program_id(1) == 0)
    def _():
        for hd in range(GDN_HEADS):
            cols = slice(hd * GDN_DK, (hd + 1) * GDN_DK)
            o = of_ref[:, cols] + ob_ref[:, cols]
            o = o * lax.rsqrt(jnp.mean(o * o, axis=-1, keepdims=True) + EPS) * gnw_ref[...]
            z = z_ref[:, cols]
            mb_ref[:, cols] = (o * (z * _sigmoid(z))).astype(BF16)

    h = h_ref[...]
    acc = None
    for m_ref, wg, pr in ((ma, wg0, pa), (mb_ref, wg1, pb), (mc, wg2, pc), (md, wg3, pd)):
        gate = _sigmoid(jnp.dot(h, wg[...], preferred_element_type=F32))
        y = jnp.dot(m_ref[...], pr[...], preferred_element_type=F32)
        acc = gate * y if acc is None else acc + gate * y
    o_ref[...] = acc.astype(BF16)


def _merge(h, m_a, o_f, o_b, p, gdn_norm_w, m_c, m_d, w_gate, projs, tm=512, tn=512):
    t = h.shape[0]
    nj = D_MODEL // tn
    tokm = pl.BlockSpec((tm, MIX_W), lambda i, j: (i, 0))
    in_specs = [pl.BlockSpec((tm, D_MODEL), lambda i, j: (i, 0)), tokm, tokm, tokm,
                pl.BlockSpec((tm, MIX_W), lambda i, j: (i, OFF_XZ // MIX_W)),
                pl.BlockSpec((1, GDN_DK), lambda i, j: (0, 0)), tokm, tokm]
    in_specs += [pl.BlockSpec((D_MODEL, tn), functools.partial(lambda i, j, b: (0, b * nj + j), b=b))
                 for b in range(N_BRANCH)]
    in_specs += [pl.BlockSpec((MIX_W, tn), lambda i, j: (0, j))] * 4
    return pl.pallas_call(
        _merge_kernel,
        grid=(t // tm, nj),
        in_specs=in_specs,
        out_specs=pl.BlockSpec((tm, tn), lambda i, j: (i, j)),
        out_shape=jax.ShapeDtypeStruct((t, D_MODEL), BF16),
        scratch_shapes=[pltpu.VMEM((tm, MIX_W), BF16)],
        compiler_params=_cparams(("parallel", "arbitrary")),
        name="merge",
    )(h, m_a, o_f, o_b, p, gdn_norm_w, m_c, m_d, w_gate, w_gate, w_gate, w_gate, *projs)


def _outproj_kernel(x_ref, m_ref, w_ref, nw_ref, x1_ref, h2_ref):
    x1 = x_ref[...] + jnp.dot(m_ref[...], w_ref[...], preferred_element_type=F32)
    x1_ref[...] = x1
    ms = jnp.mean(x1 * x1, axis=-1, keepdims=True)
    h2_ref[...] = ((x1 * lax.rsqrt(ms + EPS)) * nw_ref[...]).astype(BF16)


def _out_proj(x2d, merged, w_out, norm_w, tm=512):
    t = x2d.shape[0]
    tok = pl.BlockSpec((tm, D_MODEL), lambda i: (i, 0))
    return pl.pallas_call(
        _outproj_kernel,
        grid=(t // tm,),
        in_specs=[tok, tok, pl.BlockSpec((D_MODEL, D_MODEL), lambda i: (0, 0)),
                  pl.BlockSpec((1, D_MODEL), lambda i: (0, 0))],
        out_specs=[tok, tok],
        out_shape=[jax.ShapeDtypeStruct((t, D_MODEL), F32), jax.ShapeDtypeStruct((t, D_MODEL), BF16)],
        compiler_params=_cparams(("parallel",)),
        name="out_proj",
    )(x2d, merged, w_out, norm_w)


def _mlp_kernel(x1_ref, h2_ref, w1_ref, w2_ref, nw_ref, o_ref, *, final, col_chunk):
    f = pl.program_id(1)

    @pl.when(f == 0)
    def _():
        o_ref[...] = x1_ref[...]

    a = jnp.maximum(jnp.dot(h2_ref[...], w1_ref[...], preferred_element_type=F32), 0.0)
    a = (a * a).astype(BF16)
    for c in range(0, D_MODEL, col_chunk):
        o_ref[:, c:c + col_chunk] += jnp.dot(a, w2_ref[:, c:c + col_chunk], preferred_element_type=F32)

    if final:
        @pl.when(f == pl.num_programs(1) - 1)
        def _():
            x2 = o_ref[...]
            ms = jnp.mean(x2 * x2, axis=-1, keepdims=True)
            o_ref[...] = (x2 * lax.rsqrt(ms + EPS)) * nw_ref[...]


def _mlp_loop_kernel(x1_ref, h2_ref, w1_hbm, w2_hbm, nw_ref, o_ref, *, final, tf):
    o_ref[...] = x1_ref[...]

    def step(w1_blk, w2_blk):
        a = jnp.maximum(jnp.dot(h2_ref[...], w1_blk[...], preferred_element_type=F32), 0.0)
        o_ref[...] += jnp.dot((a * a).astype(BF16), w2_blk[...], preferred_element_type=F32)

    pltpu.emit_pipeline(
        step, grid=(D_FF // tf,),
        in_specs=[pl.BlockSpec((D_MODEL, tf), lambda f: (0, f), pipeline_mode=pl.Buffered(3)),
                  pl.BlockSpec((tf, D_MODEL), lambda f: (f, 0), pipeline_mode=pl.Buffered(3))],
    )(w1_hbm, w2_hbm)
    if final:
        x2 = o_ref[...]
        ms = jnp.mean(x2 * x2, axis=-1, keepdims=True)
        o_ref[...] = (x2 * lax.rsqrt(ms + EPS)) * nw_ref[...]


def _mlp(x1, h2, w1, w2, norm_w, final, tm=512, tf=1024, col_chunk=512):
    t = x1.shape[0]
    tok1 = pl.BlockSpec((tm, D_MODEL), lambda i: (i, 0))
    return pl.pallas_call(
        functools.partial(_mlp_loop_kernel, final=final, tf=tf),
        grid=(t // tm,),
        in_specs=[tok1, tok1, pl.BlockSpec(memory_space=pl.ANY), pl.BlockSpec(memory_space=pl.ANY),
                  pl.BlockSpec((1, D_MODEL), lambda i: (0, 0))],
        out_specs=tok1,
        out_shape=jax.ShapeDtypeStruct((t, D_MODEL), F32),
        compiler_params=_cparams(("arbitrary",)),
        name="mlp",
    )(x1, h2, w1, w2, norm_w)


def _mlp_grid(x1, h2, w1, w2, norm_w, final, tm=512, tf=1024, col_chunk=512):
    t = x1.shape[0]
    tok = pl.BlockSpec((tm, D_MODEL), lambda i, f: (i, 0))
    return pl.pallas_call(
        functools.partial(_mlp_kernel, final=final, col_chunk=col_chunk),
        grid=(t // tm, D_FF // tf),
        in_specs=[tok, tok,
                  pl.BlockSpec((D_MODEL, tf), lambda i, f: (0, f)),
                  pl.BlockSpec((tf, D_MODEL), lambda i, f: (f, 0)),
                  pl.BlockSpec((1, D_MODEL), lambda i, f: (0, 0))],
        out_specs=tok,
        out_shape=jax.ShapeDtypeStruct((t, D_MODEL), F32),
        compiler_params=_cparams(("parallel", "arbitrary")),
        name="mlp",
    )(x1, h2, w1, w2, norm_w)


def _prep_layer(l, norm_mix_w, w_in, pool_w, pool_scale, pool_proj, gdn_conv_w, gdn_a_log, gdn_dt_bias,
                gdn_norm_w, gdn_proj, conf_dw_w, conf_dw_b, conf_ln_w, conf_ln_b, conf_proj, sgu_ln_w,
                sgu_ln_b, sgu_w, sgu_b, sgu_proj, w_out, norm_mlp_w, mlp_w1, mlp_w2):
    w = w_in[l]
    o_a, o_qkv, o_z, o_ab, o_c, o_d, o_g = 0, MIX_W, 4 * MIX_W, 5 * MIX_W, 5 * MIX_W + 16, 7 * MIX_W + 16, 9 * MIX_W + 16
    w_mix = jnp.concatenate([
        w[:, o_qkv:o_z], w[:, o_a:o_qkv], w[:, o_c:o_d], w[:, o_d:o_g], w[:, o_z:o_ab], w[:, o_ab:o_c],
        jnp.zeros((D_MODEL, D_MIXP - OFF_AB - 16), F32)], axis=1).astype(BF16)
    row = lambda a: a.reshape(1, -1).astype(F32)
    pad_row = lambda a: jnp.zeros((1, LANE), F32).at[0, 2 * GDN_HEADS:4 * GDN_HEADS].set(a.reshape(-1))
    return dict(
        norm_mix_w=row(norm_mix_w[l]),
        w_mix=w_mix,
        w_gate=w[:, o_g:].astype(BF16),
        pool_w=pool_w[l].astype(BF16),
        pool_scale=row(pool_scale[l]),
        gconv_w=gdn_conv_w[l].astype(F32),
        ab_a=pad_row(-jnp.exp(gdn_a_log[l].astype(F32))),
        ab_dtb=pad_row(gdn_dt_bias[l].astype(F32)),
        gdn_norm_w=row(gdn_norm_w[l]),
        cdw_w=conf_dw_w[l].astype(F32),
        cdw_b=row(conf_dw_b[l]),
        cln_w=row(conf_ln_w[l]),
        cln_b=row(conf_ln_b[l]),
        sln_w=row(sgu_ln_w[l]),
        sln_b=row(sgu_ln_b[l]),
        sgu_w=sgu_w[l].astype(BF16),
        sgu_bias=jnp.repeat(sgu_b[l].astype(F32).T, GROUP_DIM, axis=1),
        projs=tuple(pr[l].astype(BF16) for pr in (pool_proj, gdn_proj, conf_proj, sgu_proj)),
        w_out=w_out[l].astype(BF16),
        norm_mlp_w=row(norm_mlp_w[l]),
        w1=mlp_w1[l].astype(BF16),
        w2=mlp_w2[l].astype(BF16),
    )


def _encoder(x, layers, norm_final_w):
    b, s, _ = x.shape
    x2d = x.reshape(b * s, D_MODEL)
    nf = norm_final_w.reshape(1, -1).astype(F32)
    for li, lw in enumerate(layers):
        p, h = _in_proj(x2d, lw["norm_mix_w"], lw["w_mix"])
        m_a, m_c, m_d, qkv, abg, abgt = _mix(p, lw, s)
        o_f, o_b = _gdn(qkv, abg, abgt, s)
        merged = _merge(h, m_a, o_f, o_b, p, lw["gdn_norm_w"], m_c, m_d, lw["w_gate"], lw["projs"])
        x1, h2 = _out_proj(x2d, merged, lw["w_out"], lw["norm_mlp_w"])
        x2d = _mlp(x1, h2, lw["w1"], lw["w2"], nf, final=(li == len(layers) - 1))
    return x2d.reshape(b, s, D_MODEL)


def kernel(x_prompt, x_sample, norm_mix_w, w_in, pool_w, pool_scale, pool_proj, gdn_conv_w, gdn_a_log,
           gdn_dt_bias, gdn_norm_w, gdn_proj, conf_dw_w, conf_dw_b, conf_ln_w, conf_ln_b, conf_proj,
           sgu_ln_w, sgu_ln_b, sgu_w, sgu_b, sgu_proj, w_out, norm_mlp_w, mlp_w1, mlp_w2, norm_final_w):
    depth = w_in.shape[0]
    layers = [_prep_layer(l, norm_mix_w, w_in, pool_w, pool_scale, pool_proj, gdn_conv_w, gdn_a_log,
                          gdn_dt_bias, gdn_norm_w, gdn_proj, conf_dw_w, conf_dw_b, conf_ln_w, conf_ln_b,
                          conf_proj, sgu_ln_w, sgu_ln_b, sgu_w, sgu_b, sgu_proj, w_out, norm_mlp_w,
                          mlp_w1, mlp_w2) for l in range(depth)]
    y_prompt = _encoder(x_prompt, layers, norm_final_w)
    y_sample = _encoder(x_sample, layers, norm_final_w)
    return (y_prompt, y_sample)
```
